```python
import math
import jax, jax.numpy as jnp
from jax import lax
import numpy as np

D_MODEL = 1024
BATCH = 32
SEQ = 2048
DEPTH = 2
DEC_BATCH = 1
DEC_SEQ = 16384
PAST_LEN = 128

N_MEM = 256
DN_HEADS = 6
DN_DK = 128
DN_DV = 128
DN_CONV = 5
DN_CHUNK = 64
WA_Q_HEADS = 12
WA_KV_HEADS = 4
WA_HEAD_DIM = 64
WINDOW = 128
WA_BLOCK = 128
ROPE_THETA = 10000.0
MX_HEADS = 4
MX_HEAD_DIM = 64
D_FF = 3584
N_EXPERTS = 8
TOP_K = 2
MOE_BLOCK = 128
EPS = 1e-6

DN_QK = DN_HEADS * DN_DK
DN_V = DN_HEADS * DN_DV
MX_W = MX_HEADS * MX_HEAD_DIM
A_IN = 2 * DN_QK + 2 * DN_V + 4 * DN_HEADS + MX_W
A_MIX = DN_V + MX_W
WA_Q = WA_Q_HEADS * WA_HEAD_DIM
WA_KV = WA_KV_HEADS * WA_HEAD_DIM
B_IN = WA_Q + 2 * WA_KV + MX_W
B_MIX = WA_Q + MX_W
N_A = (DEPTH + 1) // 2
N_B = DEPTH // 2

kernel_name = "hybrid_deltanet_swa_memory_encoder"

F32 = jnp.float32


def rmsnorm(x, g):
    xf = x.astype(F32)
    y = xf * lax.rsqrt(jnp.mean(xf * xf, axis=-1, keepdims=True) + EPS)
    return (y * g.astype(F32)).astype(x.dtype)


def l2norm(x):
    xf = x.astype(F32)
    return xf * lax.rsqrt(jnp.sum(xf * xf, axis=-1, keepdims=True) + EPS)


def rotary(x):
    L, Dh = x.shape[1], x.shape[-1]
    inv = ROPE_THETA ** (-jnp.arange(0, Dh, 2, dtype=F32) / Dh)
    ang = jnp.arange(L, dtype=F32)[:, None] * inv[None, :]
    cos = jnp.cos(ang)[None, :, None, :]
    sin = jnp.sin(ang)[None, :, None, :]
    xf = x.astype(F32)
    x1, x2 = xf[..., : Dh // 2], xf[..., Dh // 2:]
    return jnp.concatenate([x1 * cos - x2 * sin, x2 * cos + x1 * sin], axis=-1).astype(x.dtype)


def swiglu(h, w_gate, w_up, w_down):
    return (jax.nn.silu(h @ w_gate) * (h @ w_up)) @ w_down


def memory_attention(qc, mem_n, w_mem_kv):
    B, M = mem_n.shape[:2]
    kv = (mem_n @ w_mem_kv).reshape(B, M, 2, MX_HEADS, MX_HEAD_DIM)
    k, v = kv[:, :, 0], kv[:, :, 1]
    s = jnp.einsum('blhd,bmhd->bhlm', qc, k).astype(F32) * (MX_HEAD_DIM ** -0.5)
    p = jax.nn.softmax(s, axis=-1).astype(v.dtype)
    return jnp.einsum('bhlm,bmhd->blhd', p, v)


def centred_depthwise_conv(x, w):
    C = x.shape[-1]
    pad = (DN_CONV - 1) // 2
    return lax.conv_general_dilated(
        x, w[:, None, :].astype(x.dtype), window_strides=(1,), padding=[(pad, pad)],
        dimension_numbers=('NWC', 'WIO', 'NWC'), feature_group_count=C)


def gated_delta_rule(q, k, v, g, beta):
    B, L, H, DK = q.shape
    DV = v.shape[-1]
    C = DN_CHUNK
    N = L // C

    def to_chunks(t):
        t = t.reshape((B, N, C, H) + t.shape[3:])
        return jnp.moveaxis(t, 3, 2)

    q, k, v, g, beta = (to_chunks(t) for t in (q, k, v, g, beta))
    G = jnp.cumsum(g, axis=-1)
    idx = jnp.arange(C)
    incl = idx[:, None] >= idx[None, :]
    strict = idx[:, None] > idx[None, :]
    decay = jnp.exp(jnp.where(incl, G[..., :, None] - G[..., None, :], -jnp.inf))
    kb = k * beta[..., None]
    lower = jnp.where(strict, jnp.einsum('bnhid,bnhjd->bnhij', kb, k) * decay, 0.0)
    rhs = jnp.concatenate([v * beta[..., None], kb * jnp.exp(G)[..., None]], axis=-1)
    uw = lax.linalg.triangular_solve(lower, rhs, left_side=True, lower=True, unit_diagonal=True)
    u, w = uw[..., :DV], uw[..., DV:]
    a_qk = jnp.einsum('bnhid,bnhjd->bnhij', q, k) * decay
    g_last = G[..., -1]
    q_dec = q * jnp.exp(G)[..., None]
    k_dec = k * jnp.exp(g_last[..., None] - G)[..., None]

    def step(S, xs):
        qd, kd, uc, wc, aqk, gl = xs
        v_new = uc - jnp.einsum('bhck,bhkv->bhcv', wc, S)
        o = jnp.einsum('bhck,bhkv->bhcv', qd, S) + jnp.einsum('bhij,bhjv->bhiv', aqk, v_new)
        S = S * jnp.exp(gl)[..., None, None] + jnp.einsum('bhck,bhcv->bhkv', kd, v_new)
        return S, o

    xs = tuple(jnp.moveaxis(t, 1, 0) for t in (q_dec, k_dec, u, w, a_qk, g_last))
    S0 = jnp.zeros((B, H, DK, DV), F32)
    _, o = lax.scan(step, S0, xs)
    return jnp.transpose(o, (1, 0, 3, 2, 4)).reshape(B, L, H, DV)


def deltanet_layer(h, mem_n, w_in, conv_w, a_log, dt_bias, o_norm, w_mem_kv, w_out):
    B, L, _ = h.shape
    proj = h @ w_in
    s1 = 2 * DN_QK + DN_V
    s2 = s1 + DN_V
    s3 = s2 + 4 * DN_HEADS
    qkv, z, gates, qc = jnp.split(proj, [s1, s2, s3], axis=-1)
    qkv = jax.nn.silu(centred_depthwise_conv(qkv, conv_w))
    q, k, v = jnp.split(qkv, [DN_QK, 2 * DN_QK], axis=-1)
    q = l2norm(q.reshape(B, L, DN_HEADS, DN_DK)) * (DN_DK ** -0.5)
    k = l2norm(k.reshape(B, L, DN_HEADS, DN_DK))
    v = v.reshape(B, L, DN_HEADS, DN_DV).astype(F32)
    gates = gates.astype(F32).reshape(B, L, 4, DN_HEADS)
    beta = jax.nn.sigmoid(gates[:, :, 0:2])
    g = -jnp.exp(a_log.astype(F32)) * jax.nn.softplus(gates[:, :, 2:4] + dt_bias.astype(F32))
    o_fwd = gated_delta_rule(q, k, v, g[:, :, 0], beta[:, :, 0])
    flip = lambda t: jnp.flip(t, axis=1)
    o_bwd = flip(gated_delta_rule(flip(q), flip(k), flip(v), flip(g[:, :, 1]), flip(beta[:, :, 1])))
    o = rmsnorm(o_fwd + o_bwd, o_norm) * jax.nn.silu(z.reshape(B, L, DN_HEADS, DN_DV).astype(F32))
    xo = memory_attention(qc.reshape(B, L, MX_HEADS, MX_HEAD_DIM), mem_n, w_mem_kv)
    mixed = jnp.concatenate([o.astype(h.dtype).reshape(B, L, DN_V), xo.reshape(B, L, MX_W)], axis=-1)
    return mixed @ w_out


def banded_sink_attention(q, k, v, sink):
    B, L = q.shape[:2]
    Wb = WA_BLOCK
    N = L // Wb
    G = WA_Q_HEADS // WA_KV_HEADS
    qb = q.reshape(B, N, Wb, WA_KV_HEADS, G, WA_HEAD_DIM)

    def band(t):
        tp = jnp.pad(t, ((0, 0), (Wb, Wb), (0, 0), (0, 0))).reshape(B, N + 2, Wb, WA_KV_HEADS, WA_HEAD_DIM)
        return jnp.concatenate([tp[:, :-2], tp[:, 1:-1], tp[:, 2:]], axis=2)

    kb, vb = band(k), band(v)
    s = jnp.einsum('bnqhgd,bnkhd->bnhgqk', qb, kb).astype(F32) * (WA_HEAD_DIM ** -0.5)
    qi = jnp.arange(Wb)[:, None]
    kj = jnp.arange(3 * Wb)[None, :]
    rel = kj - Wb - qi
    kpos = jnp.arange(N)[:, None, None] * Wb + kj[None] - Wb
    mask = (jnp.abs(rel)[None] <= WINDOW) & (kpos >= 0) & (kpos < L)
    s = jnp.where(mask[None, :, None, None], s, -jnp.inf)
    sk = sink.astype(F32).reshape(WA_KV_HEADS, G)[None, None, :, :, None, None]
    m = jnp.maximum(jnp.max(s, axis=-1, keepdims=True), sk)
    p = jnp.exp(s - m)
    p = p / (jnp.sum(p, axis=-1, keepdims=True) + jnp.exp(sk - m))
    o = jnp.einsum('bnhgqk,bnkhd->bnqhgd', p.astype(v.dtype), vb)
    return o.reshape(B, L, WA_Q_HEADS, WA_HEAD_DIM)


def window_layer(h, mem_n, w_in, sink, w_mem_kv, w_out):
    B, L, _ = h.shape
    proj = h @ w_in
    q, k, v, qc = jnp.split(proj, [WA_Q, WA_Q + WA_KV, WA_Q + 2 * WA_KV], axis=-1)
    q = rotary(q.reshape(B, L, WA_Q_HEADS, WA_HEAD_DIM))
    k = rotary(k.reshape(B, L, WA_KV_HEADS, WA_HEAD_DIM))
    v = v.reshape(B, L, WA_KV_HEADS, WA_HEAD_DIM)
    o = banded_sink_attention(q, k, v, sink)
    xo = memory_attention(qc.reshape(B, L, MX_HEADS, MX_HEAD_DIM), mem_n, w_mem_kv)
    mixed = jnp.concatenate([o.reshape(B, L, WA_Q), xo.reshape(B, L, MX_W)], axis=-1)
    return mixed @ w_out


def moe_swiglu(h, w_router, w_gate, w_up, w_down):
    B, L, D = h.shape
    x = h.reshape(-1, D)
    T = x.shape[0]
    A = T * TOP_K
    logits = (x @ w_router).astype(F32)
    top_val, top_idx = lax.top_k(logits, TOP_K)
    gates = jax.nn.softmax(top_val, axis=-1)
    e_flat = top_idx.reshape(-1)
    t_flat = jnp.repeat(jnp.arange(T, dtype=jnp.int32), TOP_K)
    gt_flat = gates.reshape(-1)
    order = jnp.argsort(e_flat)
    e_sorted = e_flat[order]
    counts = jnp.bincount(e_flat, length=N_EXPERTS)
    padded = (counts + MOE_BLOCK - 1) // MOE_BLOCK * MOE_BLOCK
    pad_end = jnp.cumsum(padded)
    pad_start = pad_end - padded
    start = jnp.cumsum(counts) - counts
    dest = pad_start[e_sorted] + jnp.arange(A) - start[e_sorted]
    n_blocks = -(-A // MOE_BLOCK) + N_EXPERTS
    n_rows = n_blocks * MOE_BLOCK
    row_tok = jnp.zeros((n_rows,), jnp.int32).at[dest].set(t_flat[order])
    row_gate = jnp.zeros((n_rows,), F32).at[dest].set(gt_flat[order])
    blk_expert = jnp.minimum(
        jnp.searchsorted(pad_end, jnp.arange(n_blocks) * MOE_BLOCK, side='right'), N_EXPERTS - 1)
    xb = x[row_tok].reshape(n_blocks, MOE_BLOCK, D)

    def expert_block(args):
        xe, e = args
        return swiglu(xe, w_gate[e], w_up[e], w_down[e])

    yb = lax.map(expert_block, (xb, blk_expert)).reshape(n_rows, D)
    y = jnp.zeros((T, D), F32).at[row_tok].add(yb.astype(F32) * row_gate[:, None])
    return y.astype(h.dtype).reshape(B, L, D)


def trunk(x, mem, g_mix, g_mem, g_ffn, g_final, a_w_in, a_conv, a_log, a_dt_bias, a_o_norm,
          a_mem_kv, a_w_out, b_w_in, b_sink, b_mem_kv, b_w_out, f_w_gate, f_w_up, f_w_down,
          e_router, e_w_gate, e_w_up, e_w_down):
    for i in range(DEPTH):
        j = i // 2
        h = rmsnorm(x, g_mix[i])
        mem_n = rmsnorm(mem, g_mem[i])
        if i % 2 == 0:
            x = x + deltanet_layer(h, mem_n, a_w_in[j], a_conv[j], a_log[j], a_dt_bias[j],
                                   a_o_norm[j], a_mem_kv[j], a_w_out[j])
            x = x + swiglu(rmsnorm(x, g_ffn[i]), f_w_gate[j], f_w_up[j], f_w_down[j])
        else:
            x = x + window_layer(h, mem_n, b_w_in[j], b_sink[j], b_mem_kv[j], b_w_out[j])
            x = x + moe_swiglu(rmsnorm(x, g_ffn[i]), e_router[j], e_w_gate[j], e_w_up[j], e_w_down[j])
    return rmsnorm(x, g_final)


def setup_inputs(seed: int = 0) -> dict:
    key = jax.random.key(seed)
    ks = iter(jax.random.split(key, 40))
    D = D_MODEL

    def nrm(shape, fan_in):
        return jax.random.normal(next(ks), shape, F32) * (fan_in ** -0.5)

    def gain(shape):
        return 1.0 + 0.05 * jax.random.normal(next(ks), shape, F32)

    x_prompt = jax.random.normal(next(ks), (BATCH, SEQ, D), F32)
    x_sample = jax.random.normal(next(ks), (DEC_BATCH, DEC_SEQ, D), F32)
    mem_prompt = jax.random.normal(next(ks), (BATCH, N_MEM, D), F32)
    mem_sample = jax.random.normal(next(ks), (DEC_BATCH, N_MEM, D), F32)
    g_mix = gain((DEPTH, D))
    g_mem = gain((DEPTH, D))
    g_ffn = gain((DEPTH, D))
    g_final = gain((D,))
    a_w_in = nrm((N_A, D, A_IN), D)
    a_conv = nrm((N_A, DN_CONV, 2 * DN_QK + DN_V), DN_CONV)
    a_log = jnp.log(jax.random.uniform(next(ks), (N_A, 2, DN_HEADS), F32, 1.0, 16.0))
    dt = jnp.exp(jax.random.uniform(next(ks), (N_A, 2, DN_HEADS), F32, math.log(1e-3), math.log(1e-1)))
    a_dt_bias = dt + jnp.log(-jnp.expm1(-dt))
    a_o_norm = gain((N_A, DN_DV))
    a_mem_kv = nrm((N_A, D, 2 * MX_W), D)
    a_w_out = nrm((N_A, A_MIX, D), A_MIX)
    b_w_in = nrm((N_B, D, B_IN), D)
    b_sink = 0.5 * jax.random.normal(next(ks), (N_B, WA_Q_HEADS), F32)
    b_mem_kv = nrm((N_B, D, 2 * MX_W), D)
    b_w_out = nrm((N_B, B_MIX, D), B_MIX)
    f_w_gate = nrm((N_A, D, D_FF), D)
    f_w_up = nrm((N_A, D, D_FF), D)
    f_w_down = nrm((N_A, D_FF, D), D_FF)
    e_router = nrm((N_B, D, N_EXPERTS), D)
    e_w_gate = nrm((N_B, N_EXPERTS, D, D_FF), D)
    e_w_up = nrm((N_B, N_EXPERTS, D, D_FF), D)
    e_w_down = nrm((N_B, N_EXPERTS, D_FF, D), D_FF)
    return {"x_prompt": x_prompt, "x_sample": x_sample, "mem_prompt": mem_prompt, "mem_sample": mem_sample,
            "g_mix": g_mix, "g_mem": g_mem, "g_ffn": g_ffn, "g_final": g_final,
            "a_w_in": a_w_in, "a_conv": a_conv, "a_log": a_log, "a_dt_bias": a_dt_bias,
            "a_o_norm": a_o_norm, "a_mem_kv": a_mem_kv, "a_w_out": a_w_out,
            "b_w_in": b_w_in, "b_sink": b_sink, "b_mem_kv": b_mem_kv, "b_w_out": b_w_out,
            "f_w_gate": f_w_gate, "f_w_up": f_w_up, "f_w_down": f_w_down,
            "e_router": e_router, "e_w_gate": e_w_gate, "e_w_up": e_w_up, "e_w_down": e_w_down}


def reference(x_prompt, x_sample, mem_prompt, mem_sample, g_mix, g_mem, g_ffn, g_final,
              a_w_in, a_conv, a_log, a_dt_bias, a_o_norm, a_mem_kv, a_w_out,
              b_w_in, b_sink, b_mem_kv, b_w_out, f_w_gate, f_w_up, f_w_down,
              e_router, e_w_gate, e_w_up, e_w_down):
    weights = (g_mix, g_mem, g_ffn, g_final, a_w_in, a_conv, a_log, a_dt_bias, a_o_norm,
               a_mem_kv, a_w_out, b_w_in, b_sink, b_mem_kv, b_w_out, f_w_gate, f_w_up, f_w_down,
               e_router, e_w_gate, e_w_up, e_w_down)
    y_prompt = trunk(x_prompt, mem_prompt, *weights)
    y_sample = trunk(x_sample, mem_sample, *weights)
    return (y_prompt, y_sample)
```

```python
import functools
import math

import jax
import jax.numpy as jnp
from jax import lax
from jax.experimental import pallas as pl
from jax.experimental.pallas import tpu as pltpu

F32 = jnp.float32
BF16 = jnp.bfloat16
I32 = jnp.int32

EPS = 1e-6
D_MODEL = 1024
LANES = 128
SUBLANES = 8
TOK_TILES = D_MODEL // LANES
VMEM_LIMIT_BYTES = 56 * 1024 * 1024

DN_HEADS = 6
DN_D = 128
DN_CONV = 5
DN_CHUNK = 64
DN_QKV = 3 * DN_HEADS * DN_D
DN_V = DN_HEADS * DN_D
DN_TILE = 256
CONV_HALO = 16

WA_Q_HEADS = 12
WA_KV_HEADS = 4
WA_GROUP = WA_Q_HEADS // WA_KV_HEADS
WA_HEAD_DIM = 64
WA_BLOCK = 128
WA_Q = WA_Q_HEADS * WA_HEAD_DIM
WA_KV = WA_KV_HEADS * WA_HEAD_DIM
ROPE_THETA = 10000.0

MX_HEADS = 4
MX_HEAD_DIM = 64
MX_W = MX_HEADS * MX_HEAD_DIM

D_FF = 3584
N_EXPERTS = 8
TOP_K = 2
FF_TILE = 512
MOE_ROWS = 512


def _params(*sem):
    return pltpu.CompilerParams(dimension_semantics=sem, vmem_limit_bytes=VMEM_LIMIT_BYTES)


def _tile(n, pref):
    t = min(n, pref)
    assert n % t == 0, (n, pref)
    return t


def _dot(a, b):
    return jnp.dot(a, b, preferred_element_type=F32)


def _dot_nt(a, b):
    return lax.dot_general(a, b, (((1,), (1,)), ((), ())), preferred_element_type=F32)


def _dot_tn(a, b):
    return lax.dot_general(a, b, (((0,), (0,)), ((), ())), preferred_element_type=F32)


def _split3(x):
    hi = x.astype(BF16)
    r1 = x - hi.astype(F32)
    mid = r1.astype(BF16)
    lo = (r1 - mid.astype(F32)).astype(BF16)
    return hi, mid, lo


def _dot_x3(a, b):
    a_hi = a.astype(BF16)
    a_lo = (a - a_hi.astype(F32)).astype(BF16)
    b_hi = b.astype(BF16)
    b_lo = (b - b_hi.astype(F32)).astype(BF16)
    return _dot(a_hi, b_hi) + _dot(a_lo, b_hi) + _dot(a_hi, b_lo)


def _dot_mask_f32(mask_bf16, x):
    hi, mid, lo = _split3(x)
    return _dot(mask_bf16, hi) + _dot(mask_bf16, mid) + _dot(mask_bf16, lo)


def _rmsnorm(x, g):
    ms = jnp.mean(x * x, axis=-1, keepdims=True)
    return x * lax.rsqrt(ms + EPS) * g


def _silu(x):
    return x * jax.nn.sigmoid(x)


def _norm_proj_kernel(n_out, x_ref, g_ref, *refs):
    w_refs, o_refs = refs[:n_out], refs[n_out:]
    hf = _rmsnorm(x_ref[...], g_ref[...])
    hb = hf.astype(BF16)
    for w_ref, o_ref in zip(w_refs, o_refs):
        if w_ref.dtype == F32:
            o_ref[...] = _dot_x3(hf, w_ref[...]).astype(o_ref.dtype)
        else:
            o_ref[...] = _dot(hb, w_ref[...]).astype(o_ref.dtype)


def norm_proj(x, g, weights, out_dtypes, tm_pref=512):
    T, D = x.shape
    tm = _tile(T, tm_pref)
    n_out = len(weights)
    in_specs = [pl.BlockSpec((tm, D), lambda i: (i, 0)), pl.BlockSpec((1, D), lambda i: (0, 0))]
    in_specs += [pl.BlockSpec(w.shape, lambda i: (0, 0)) for w in weights]
    out_specs = [pl.BlockSpec((tm, w.shape[1]), lambda i: (i, 0)) for w in weights]
    out_shape = [jax.ShapeDtypeStruct((T, w.shape[1]), dt) for w, dt in zip(weights, out_dtypes)]
    return pl.pallas_call(
        functools.partial(_norm_proj_kernel, n_out),
        grid=(T // tm,), in_specs=in_specs, out_specs=out_specs, out_shape=out_shape,
        compiler_params=_params("parallel"), name="norm_proj",
    )(x, g.reshape(1, D), *weights)


def _dn_prep_kernel(tl, cur_ref, prev_ref, next_ref, gates_ref, cw_ref, alog_ref, dtb_ref,
                    q_ref, k_ref, v_ref, pack_ref, packt_ref, ext_ref):
    i = pl.program_id(1)
    nt = pl.num_programs(1)
    has_prev = (i > 0).astype(F32)
    has_next = (i < nt - 1).astype(F32)
    ext_ref[0:CONV_HALO, :] = prev_ref[...].astype(F32) * has_prev
    ext_ref[CONV_HALO:CONV_HALO + tl, :] = cur_ref[...].astype(F32)
    ext_ref[CONV_HALO + tl:, :] = next_ref[...].astype(F32) * has_next
    pad = (DN_CONV - 1) // 2
    for cg in range(DN_QKV // LANES):
        sl = slice(cg * LANES, (cg + 1) * LANES)
        acc = None
        for j in range(DN_CONV):
            term = ext_ref[pl.ds(CONV_HALO - pad + j, tl), sl] * cw_ref[j:j + 1, sl]
            acc = term if acc is None else acc + term
        y = _silu(acc)
        if cg < DN_HEADS:
            y = y * lax.rsqrt(jnp.sum(y * y, axis=-1, keepdims=True) + EPS) * (DN_D ** -0.5)
            q_ref[cg] = y.astype(q_ref.dtype)
        elif cg < 2 * DN_HEADS:
            y = y * lax.rsqrt(jnp.sum(y * y, axis=-1, keepdims=True) + EPS)
            k_ref[cg - DN_HEADS] = y.astype(k_ref.dtype)
        else:
            v_ref[cg - 2 * DN_HEADS] = y.astype(v_ref.dtype)

    gt = gates_ref[...]
    lane = lax.broadcasted_iota(I32, gt.shape, 1)
    beta = jax.nn.sigmoid(gt)
    xg = gt + dtb_ref[...]
    softplus = jnp.maximum(xg, 0.0) + jnp.log1p(jnp.exp(-jnp.abs(xg)))
    g = -jnp.exp(alog_ref[...]) * softplus
    g = jnp.where((lane >= 2 * DN_HEADS) & (lane < 4 * DN_HEADS), g, 0.0)
    r = lax.broadcasted_iota(I32, (tl, tl), 0)
    c = lax.broadcasted_iota(I32, (tl, tl), 1)
    same = (r // DN_CHUNK) == (c // DN_CHUNK)
    m_fwd = (same & (r >= c)).astype(BF16)
    m_bwd = (same & (r <= c)).astype(BF16)
    hi, mid, lo = _split3(g)
    g_fwd = _dot(m_fwd, hi) + _dot(m_fwd, mid) + _dot(m_fwd, lo)
    g_bwd = _dot(m_bwd, hi) + _dot(m_bwd, mid) + _dot(m_bwd, lo)
    pack = jnp.where(lane < 2 * DN_HEADS, beta, jnp.where(lane < 3 * DN_HEADS, g_fwd, g_bwd))
    pack_ref[...] = pack
    packt_ref[...] = pack.T


def dn_prep(qkv, gates, conv_w, a_log, dt_bias):
    B, L, _ = qkv.shape
    tl = _tile(L, DN_TILE)
    nt = L // tl
    hb = tl // CONV_HALO
    nhalo = L // CONV_HALO
    cw = jnp.zeros((SUBLANES, DN_QKV), F32).at[:DN_CONV].set(conv_w)
    alog = jnp.zeros((1, LANES), F32).at[0, 2 * DN_HEADS:4 * DN_HEADS].set(a_log.reshape(-1))
    dtb = jnp.zeros((1, LANES), F32).at[0, 2 * DN_HEADS:4 * DN_HEADS].set(dt_bias.reshape(-1))
    hm = jax.ShapeDtypeStruct((B, DN_HEADS, L, DN_D), BF16)
    hm_spec = pl.BlockSpec((None, DN_HEADS, tl, DN_D), lambda b, i: (b, 0, i, 0))
    return pl.pallas_call(
        functools.partial(_dn_prep_kernel, tl),
        grid=(B, nt),
        in_specs=[
            pl.BlockSpec((None, tl, DN_QKV), lambda b, i: (b, i, 0)),
            pl.BlockSpec((None, CONV_HALO, DN_QKV), lambda b, i: (b, jnp.maximum(i * hb - 1, 0), 0)),
            pl.BlockSpec((None, CONV_HALO, DN_QKV), lambda b, i: (b, jnp.minimum((i + 1) * hb, nhalo - 1), 0)),
            pl.BlockSpec((None, tl, LANES), lambda b, i: (b, i, 0)),
            pl.BlockSpec((SUBLANES, DN_QKV), lambda b, i: (0, 0)),
            pl.BlockSpec((1, LANES), lambda b, i: (0, 0)),
            pl.BlockSpec((1, LANES), lambda b, i: (0, 0)),
        ],
        out_specs=[hm_spec, hm_spec, hm_spec,
                   pl.BlockSpec((None, tl, LANES), lambda b, i: (b, i, 0)),
                   pl.BlockSpec((None, LANES, tl), lambda b, i: (b, 0, i))],
        out_shape=[hm, hm, hm, jax.ShapeDtypeStruct((B, L, LANES), F32),
                   jax.ShapeDtypeStruct((B, LANES, L), F32)],
        scratch_shapes=[pltpu.VMEM((tl + 2 * CONV_HALO, DN_QKV), F32)],
        compiler_params=_params("parallel", "parallel"), name="dn_prep",
    )(qkv, qkv, qkv, gates, cw, alog, dtb)


def _dn_unit(d, h, tl, q_ref, k_ref, v_ref, pack_ref, packt_ref, o_ref, s_ref, masks):
    incl, strict, eye = masks
    nchunk = tl // DN_CHUNK
    pk = pack_ref[...]
    lane = lax.broadcasted_iota(I32, pk.shape, 1)
    beta_c = jnp.sum(jnp.where(lane == d * DN_HEADS + h, pk, 0.0), axis=1, keepdims=True)
    li_g = (2 + d) * DN_HEADS + h
    g_c = jnp.sum(jnp.where(lane == li_g, pk, 0.0), axis=1, keepdims=True)
    g_r = packt_ref[pl.ds(li_g, 1), :]
    q16, k16 = q_ref[h], k_ref[h]
    qf, kf, vf = q16.astype(F32), k16.astype(F32), v_ref[h].astype(F32)
    decay = jnp.exp(jnp.where(incl, g_c - g_r, -jnp.inf))
    kb = kf * beta_c
    a = jnp.where(strict, _dot_nt(kb.astype(BF16), k16) * decay, 0.0)
    a_qk = (_dot_nt(q16, k16) * decay).astype(BF16)
    t_inv = eye - a
    p = a
    for _ in range(int(math.log2(DN_CHUNK)) - 1):
        p = _dot_x3(p, p)
        t_inv = t_inv + _dot_x3(t_inv, p)
    e_g = jnp.exp(g_c)
    rhs = jnp.concatenate([vf * beta_c, kb * e_g], axis=1)
    uw = _dot_x3(t_inv, rhs)
    u, w = uw[:, :DN_D], uw[:, DN_D:].astype(BF16)
    qd = (qf * e_g).astype(BF16)
    s = s_ref[d * DN_HEADS + h]
    order = range(nchunk) if d == 0 else range(nchunk - 1, -1, -1)
    for c in order:
        rs = slice(c * DN_CHUNK, (c + 1) * DN_CHUNK)
        edge = (c + 1) * DN_CHUNK - 1 if d == 0 else c * DN_CHUNK
        g_last = g_r[:, edge:edge + 1]
        kd = (kf[rs] * jnp.exp(g_last - g_c[rs])).astype(BF16)
        sb = s.astype(BF16)
        v_new = (u[rs] - _dot(w[rs], sb)).astype(BF16)
        o = _dot(qd[rs], sb) + _dot(a_qk[rs, rs], v_new)
        s = s * jnp.exp(g_last) + _dot_tn(kd, v_new)
        o_ref[h, rs, :] = o.astype(o_ref.dtype)
    s_ref[d * DN_HEADS + h] = s


def _dn_kernel(tl, qf_ref, kf_ref, vf_ref, pf_ref, ptf_ref, qb_ref, kb_ref, vb_ref, pb_ref, ptb_ref,
               of_ref, ob_ref, s_ref):
    @pl.when(pl.program_id(1) == 0)
    def _():
        s_ref[...] = jnp.zeros_like(s_ref)

    r = lax.broadcasted_iota(I32, (tl, tl), 0)
    c = lax.broadcasted_iota(I32, (tl, tl), 1)
    same = (r // DN_CHUNK) == (c // DN_CHUNK)
    eye = (r == c).astype(F32)
    masks_f = (same & (r >= c), same & (r > c), eye)
    masks_b = (same & (r <= c), same & (r < c), eye)

    def head(h, carry):
        _dn_unit(0, h, tl, qf_ref, kf_ref, vf_ref, pf_ref, ptf_ref, of_ref, s_ref, masks_f)
        _dn_unit(1, h, tl, qb_ref, kb_ref, vb_ref, pb_ref, ptb_ref, ob_ref, s_ref, masks_b)
        return carry

    lax.fori_loop(0, DN_HEADS, head, 0)


def delta_rule(q, k, v, pack, packt):
    B, H, L, _ = q.shape
    tl = _tile(L, DN_TILE)
    nt = L // tl
    fwd = lambda b, i: (b, 0, i, 0)
    bwd = lambda b, i: (b, 0, nt - 1 - i, 0)
    hm_f = pl.BlockSpec((None, H, tl, DN_D), fwd)
    hm_b = pl.BlockSpec((None, H, tl, DN_D), bwd)
    pk_f = pl.BlockSpec((None, tl, LANES), lambda b, i: (b, i, 0))
    pk_b = pl.BlockSpec((None, tl, LANES), lambda b, i: (b, nt - 1 - i, 0))
    pt_f = pl.BlockSpec((None, LANES, tl), lambda b, i: (b, 0, i))
    pt_b = pl.BlockSpec((None, LANES, tl), lambda b, i: (b, 0, nt - 1 - i))
    o = jax.ShapeDtypeStruct((B, H, L, DN_D), BF16)
    return pl.pallas_call(
        functools.partial(_dn_kernel, tl),
        grid=(B, nt),
        in_specs=[hm_f, hm_f, hm_f, pk_f, pt_f, hm_b, hm_b, hm_b, pk_b, pt_b],
        out_specs=[hm_f, hm_b], out_shape=[o, o],
        scratch_shapes=[pltpu.VMEM((2 * H, DN_D, DN_D), F32)],
        compiler_params=_params("parallel", "arbitrary"), name="delta_rule",
    )(q, k, v, pack, packt, q, k, v, pack, packt)


def _mem_attn_kernel(q_ref, kv_ref, o_ref):
    q = q_ref[...]
    kv = kv_ref[...]
    outs = []
    for h in range(MX_HEADS):
        hs = slice(h * MX_HEAD_DIM, (h + 1) * MX_HEAD_DIM)
        vs = slice(MX_W + h * MX_HEAD_DIM, MX_W + (h + 1) * MX_HEAD_DIM)
        s = _dot_nt(q[:, hs], kv[:, hs]) * (MX_HEAD_DIM ** -0.5)
        m = jnp.max(s, axis=-1, keepdims=True)
        p = jnp.exp(s - m)
        den = jnp.sum(p, axis=-1, keepdims=True)
        outs.append(_dot(p.astype(BF16), kv[:, vs]) / den)
    o_ref[...] = jnp.concatenate(outs, axis=1).astype(o_ref.dtype)


def mem_attention(qc, kv, tl_pref=512):
    B, L, _ = qc.shape
    M = kv.shape[1]
    tl = _tile(L, tl_pref)
    return pl.pallas_call(
        _mem_attn_kernel, grid=(B, L // tl),
        in_specs=[pl.BlockSpec((None, tl, MX_W), lambda b, i: (b, i, 0)),
                  pl.BlockSpec((None, M, 2 * MX_W), lambda b, i: (b, 0, 0))],
        out_specs=pl.BlockSpec((None, tl, MX_W), lambda b, i: (b, i, 0)),
        out_shape=jax.ShapeDtypeStruct((B, L, MX_W), BF16),
        compiler_params=_params("parallel", "parallel"), name="mem_attention",
    )(qc, kv)


def _out_proj_a_kernel(of_ref, ob_ref, z_ref, xo_ref, x_ref, on_ref, w_ref, o_ref):
    parts = []
    z = z_ref[...].astype(F32)
    for h in range(DN_HEADS):
        o = of_ref[h].astype(F32) + ob_ref[h].astype(F32)
        y = _rmsnorm(o, on_ref[...]) * _silu(z[:, h * DN_D:(h + 1) * DN_D])
        parts.append(y.astype(BF16))
    parts.append(xo_ref[...])
    mixed = jnp.concatenate(parts, axis=1)
    o_ref[...] = x_ref[...] + _dot(mixed, w_ref[...])


def out_proj_a(o_f, o_b, z, xo, x, o_norm, w_out, tm_pref=512):
    B, H, L, _ = o_f.shape
    D = x.shape[-1]
    tm = _tile(L, tm_pref)
    hm = pl.BlockSpec((None, H, tm, DN_D), lambda b, i: (b, 0, i, 0))
    row = lambda w: pl.BlockSpec((None, tm, w), lambda b, i: (b, i, 0))
    return pl.pallas_call(
        _out_proj_a_kernel, grid=(B, L // tm),
        in_specs=[hm, hm, row(DN_V), row(MX_W), row(D),
                  pl.BlockSpec((1, DN_D), lambda b, i: (0, 0)),
                  pl.BlockSpec(w_out.shape, lambda b, i: (0, 0))],
        out_specs=row(D), out_shape=jax.ShapeDtypeStruct((B, L, D), F32),
        compiler_params=_params("parallel", "parallel"), name="out_proj_a",
    )(o_f, o_b, z, xo, x, o_norm.reshape(1, DN_D), w_out)


def _out_proj_b_kernel(o_ref, xo_ref, x_ref, w_ref, out_ref):
    mixed = jnp.concatenate([o_ref[...], xo_ref[...]], axis=1)
    out_ref[...] = x_ref[...] + _dot(mixed, w_ref[...])


def out_proj_b(o, xo, x, w_out, tm_pref=512):
    B, L, D = x.shape
    tm = _tile(L, tm_pref)
    row = lambda w: pl.BlockSpec((None, tm, w), lambda b, i: (b, i, 0))
    return pl.pallas_call(
        _out_proj_b_kernel, grid=(B, L // tm),
        in_specs=[row(WA_Q), row(MX_W), row(D), pl.BlockSpec(w_out.shape, lambda b, i: (0, 0))],
        out_specs=row(D), out_shape=jax.ShapeDtypeStruct((B, L, D), F32),
        compiler_params=_params("parallel", "parallel"), name="out_proj_b",
    )(o, xo, x, w_out)


def _swiglu_step(h, wg_ref, wu_ref, wd_ref, acc_ref):
    gate = _dot(h, wg_ref[...])
    up = _dot(h, wu_ref[...])
    acc_ref[...] += _dot((_silu(gate) * up).astype(BF16), wd_ref[...])


def _ffn_kernel(x_ref, g_ref, wg_ref, wu_ref, wd_ref, o_ref, h_ref, acc_ref):
    j = pl.program_id(1)

    @pl.when(j == 0)
    def _():
        h_ref[...] = _rmsnorm(x_ref[...], g_ref[...]).astype(BF16)
        acc_ref[...] = jnp.zeros_like(acc_ref)

    _swiglu_step(h_ref[...], wg_ref, wu_ref, wd_ref, acc_ref)

    @pl.when(j == pl.num_programs(1) - 1)
    def _():
        o_ref[...] = x_ref[...] + acc_ref[...]


def ffn(x, g, w_gate, w_up, w_down, tm_pref=1024):
    T, D = x.shape
    F = w_gate.shape[1]
    tm = _tile(T, tm_pref)
    tf = _tile(F, FF_TILE)
    return pl.pallas_call(
        _ffn_kernel, grid=(T // tm, F // tf),
        in_specs=[pl.BlockSpec((tm, D), lambda i, j: (i, 0)),
                  pl.BlockSpec((1, D), lambda i, j: (0, 0)),
                  pl.BlockSpec((D, tf), lambda i, j: (0, j)),
                  pl.BlockSpec((D, tf), lambda i, j: (0, j)),
                  pl.BlockSpec((tf, D), lambda i, j: (j, 0))],
        out_specs=pl.BlockSpec((tm, D), lambda i, j: (i, 0)),
        out_shape=jax.ShapeDtypeStruct((T, D), F32),
        scratch_shapes=[pltpu.VMEM((tm, D), BF16), pltpu.VMEM((tm, D), F32)],
        compiler_params=_params("parallel", "arbitrary"), name="ffn",
    )(x, g.reshape(1, D), w_gate, w_up, w_down)


def _rope(x, cos, sin_signed):
    n, w = x.shape
    lane = lax.broadcasted_iota(I32, (n, LANES), 1)
    first_half = (lane % WA_HEAD_DIM) < (WA_HEAD_DIM // 2)
    outs = []
    for gidx in range(w // LANES):
        xs = x[:, gidx * LANES:(gidx + 1) * LANES]
        rot = jnp.where(first_half, pltpu.roll(xs, LANES - WA_HEAD_DIM // 2, 1),
                        pltpu.roll(xs, WA_HEAD_DIM // 2, 1))
        outs.append(xs * cos + rot * sin_signed)
    return jnp.concatenate(outs, axis=1)


def _win_attn_kernel(sink_ref, q_ref, kp_ref, kc_ref, kn_ref, vp_ref, vc_ref, vn_ref,
                     cp_ref, cc_ref, cn_ref, sp_ref, sc_ref, sn_ref, o_ref):
    i = pl.program_id(1)
    n = pl.num_programs(1)
    wb = WA_BLOCK
    q = _rope(q_ref[...].astype(F32), cc_ref[...], sc_ref[...]).astype(BF16)
    k = jnp.concatenate([
        _rope(kp_ref[...].astype(F32), cp_ref[...], sp_ref[...]),
        _rope(kc_ref[...].astype(F32), cc_ref[...], sc_ref[...]),
        _rope(kn_ref[...].astype(F32), cn_ref[...], sn_ref[...])], axis=0).astype(BF16)
    v = jnp.concatenate([vp_ref[...], vc_ref[...], vn_ref[...]], axis=0)
    rows = WA_GROUP * wb
    qi = lax.broadcasted_iota(I32, (rows, 3 * wb), 0) % wb
    kj = lax.broadcasted_iota(I32, (rows, 3 * wb), 1)
    rel = kj - wb - qi
    blk = kj // wb
    valid = (jnp.abs(rel) <= WA_BLOCK) & ((blk != 0) | (i > 0)) & ((blk != 2) | (i < n - 1))
    grp = lax.broadcasted_iota(I32, (rows, 1), 0) // wb
    outs = [None] * WA_Q_HEADS
    for h in range(WA_KV_HEADS):
        hs = slice(h * WA_HEAD_DIM, (h + 1) * WA_HEAD_DIM)
        qs = jnp.concatenate(
            [q[:, (h * WA_GROUP + g) * WA_HEAD_DIM:(h * WA_GROUP + g + 1) * WA_HEAD_DIM]
             for g in range(WA_GROUP)], axis=0)
        sk = jnp.zeros((rows, 1), F32)
        for g in range(WA_GROUP):
            sk = jnp.where(grp == g, sink_ref[h * WA_GROUP + g], sk)
        s = _dot_nt(qs, k[:, hs]) * (WA_HEAD_DIM ** -0.5)
        s = jnp.where(valid, s, -jnp.inf)
        m = jnp.maximum(jnp.max(s, axis=-1, keepdims=True), sk)
        p = jnp.exp(s - m)
        den = jnp.sum(p, axis=-1, keepdims=True) + jnp.exp(sk - m)
        o = _dot(p.astype(BF16), v[:, hs]) / den
        for g in range(WA_GROUP):
            outs[h * WA_GROUP + g] = o[g * wb:(g + 1) * wb]
    o_ref[...] = jnp.concatenate(outs, axis=1).astype(o_ref.dtype)


def window_attention(q, k, v, sink, cos, sin_signed):
    B, L, _ = q.shape
    wb = WA_BLOCK
    n = L // wb
    prev = lambda b, i, s: (b, jnp.maximum(i - 1, 0), 0)
    cur = lambda b, i, s: (b, i, 0)
    nxt = lambda b, i, s: (b, jnp.minimum(i + 1, n - 1), 0)
    kv_spec = lambda f: pl.BlockSpec((None, wb, WA_KV), f)
    tab = lambda f: pl.BlockSpec((wb, LANES), lambda b, i, s: f(b, i, s)[1:])
    grid_spec = pltpu.PrefetchScalarGridSpec(
        num_scalar_prefetch=1, grid=(B, n),
        in_specs=[pl.BlockSpec((None, wb, WA_Q), cur),
                  kv_spec(prev), kv_spec(cur), kv_spec(nxt),
                  kv_spec(prev), kv_spec(cur), kv_spec(nxt),
                  tab(prev), tab(cur), tab(nxt), tab(prev), tab(cur), tab(nxt)],
        out_specs=pl.BlockSpec((None, wb, WA_Q), cur))
    return pl.pallas_call(
        _win_attn_kernel, grid_spec=grid_spec,
        out_shape=jax.ShapeDtypeStruct((B, L, WA_Q), BF16),
        compiler_params=_params("parallel", "parallel"), name="window_attention",
    )(sink, q, k, k, k, v, v, v, cos, cos, cos, sin_signed, sin_signed, sin_signed)


def _rope_tables(L):
    inv = ROPE_THETA ** (-jnp.arange(0, WA_HEAD_DIM, 2, dtype=F32) / WA_HEAD_DIM)
    ang = jnp.arange(L, dtype=F32)[:, None] * inv[None, :]
    cos, sin = jnp.cos(ang), jnp.sin(ang)
    reps = LANES // WA_HEAD_DIM
    return jnp.tile(jnp.concatenate([cos, cos], axis=1), (1, reps)), \
        jnp.tile(jnp.concatenate([-sin, sin], axis=1), (1, reps))


def _router_kernel(tm, x_ref, g_ref, wr_ref, hs_ref, mi_ref, mf_ref, cnt_ref, carry_ref):
    @pl.when(pl.program_id(0) == 0)
    def _():
        carry_ref[...] = jnp.zeros_like(carry_ref)

    hf = _rmsnorm(x_ref[...], g_ref[...])
    for s in range(TOK_TILES):
        hs_ref[pl.ds(s, tm, stride=TOK_TILES), :] = hf[:, s * LANES:(s + 1) * LANES]
    logits = _dot_x3(hf, wr_ref[...])
    lane = lax.broadcasted_iota(I32, logits.shape, 1)
    lg = jnp.where(lane < N_EXPERTS, logits, -jnp.inf)
    m1 = jnp.max(lg, axis=1, keepdims=True)
    i1 = jnp.min(jnp.where(lg == m1, lane, LANES), axis=1, keepdims=True)
    lg2 = jnp.where(lane == i1, -jnp.inf, lg)
    m2 = jnp.max(lg2, axis=1, keepdims=True)
    i2 = jnp.min(jnp.where(lg2 == m2, lane, LANES), axis=1, keepdims=True)
    e2 = jnp.exp(m2 - m1)
    g1 = 1.0 / (1.0 + e2)
    g2 = e2 / (1.0 + e2)
    oh1 = lane == i1
    oh2 = lane == i2
    oh = (oh1 | oh2).astype(F32)
    r = lax.broadcasted_iota(I32, (tm, tm), 0)
    c = lax.broadcasted_iota(I32, (tm, tm), 1)
    before = _dot((r > c).astype(BF16), oh.astype(BF16)) + carry_ref[...]
    rank1 = jnp.sum(jnp.where(oh1, before, 0.0), axis=1, keepdims=True).astype(I32)
    rank2 = jnp.sum(jnp.where(oh2, before, 0.0), axis=1, keepdims=True).astype(I32)
    carry_ref[...] += jnp.sum(oh, axis=0, keepdims=True)
    mi_ref[...] = jnp.where(lane == 0, i1, jnp.where(lane == 1, i2, jnp.where(lane == 2, rank1,
                            jnp.where(lane == 3, rank2, 0))))
    mf_ref[...] = jnp.where(lane == 0, g1, jnp.where(lane == 1, g2, 0.0))
    cnt_ref[...] = jnp.broadcast_to(carry_ref[...], cnt_ref.shape)


def moe_router(x, g, w_router, tm_pref=512):
    T, D = x.shape
    tm = _tile(T, tm_pref)
    wr = jnp.zeros((D, LANES), F32).at[:, :N_EXPERTS].set(w_router)
    return pl.pallas_call(
        functools.partial(_router_kernel, tm), grid=(T // tm,),
        in_specs=[pl.BlockSpec((tm, D), lambda i: (i, 0)),
                  pl.BlockSpec((1, D), lambda i: (0, 0)),
                  pl.BlockSpec((D, LANES), lambda i: (0, 0))],
        out_specs=[pl.BlockSpec((tm * TOK_TILES, LANES), lambda i: (i, 0)),
                   pl.BlockSpec((tm, LANES), lambda i: (i, 0)),
                   pl.BlockSpec((tm, LANES), lambda i: (i, 0)),
                   pl.BlockSpec((SUBLANES, LANES), lambda i: (0, 0))],
        out_shape=[jax.ShapeDtypeStruct((T * TOK_TILES, LANES), F32),
                   jax.ShapeDtypeStruct((T, LANES), I32),
                   jax.ShapeDtypeStruct((T, LANES), F32),
                   jax.ShapeDtypeStruct((SUBLANES, LANES), F32)],
        scratch_shapes=[pltpu.VMEM((1, LANES), F32)],
        compiler_params=_params("arbitrary"), name="moe_router",
    )(x, g.reshape(1, D), wr)


def _row_copy(src, src_row, dst, dst_row, sem):
    return pltpu.make_async_copy(src.at[pl.ds(src_row * TOK_TILES, TOK_TILES), :],
                                 dst.at[pl.ds(dst_row * TOK_TILES, TOK_TILES), :], sem)


def _dispatch_kernel(tm, dest_ref, hs_ref, xs_in_ref, xs_ref, sem):
    del xs_in_ref
    base = pl.program_id(0) * tm

    def issue(t, carry):
        for k in range(TOP_K):
            _row_copy(hs_ref, base + t, xs_ref, dest_ref[k, t], sem).start()
        return carry

    lax.fori_loop(0, tm, issue, 0)
    n = TOP_K * tm * TOK_TILES
    pltpu.make_async_copy(hs_ref.at[pl.ds(0, n), :], xs_ref.at[pl.ds(0, n), :], sem).wait()


def moe_dispatch(hs, dest, n_rows, tm_pref=1024):
    T = dest.shape[1]
    tm = _tile(T, tm_pref)
    xs0 = jnp.zeros((n_rows * TOK_TILES, LANES), F32)
    return pl.pallas_call(
        functools.partial(_dispatch_kernel, tm), grid=(T // tm,),
        in_specs=[pl.BlockSpec((TOP_K, tm), lambda i: (0, i), memory_space=pltpu.SMEM),
                  pl.BlockSpec(memory_space=pl.ANY), pl.BlockSpec(memory_space=pl.ANY)],
        out_specs=pl.BlockSpec(memory_space=pl.ANY),
        out_shape=jax.ShapeDtypeStruct(xs0.shape, F32),
        scratch_shapes=[pltpu.SemaphoreType.DMA(())],
        input_output_aliases={2: 0},
        compiler_params=_params("arbitrary"), name="moe_dispatch",
    )(dest, hs, xs0)


def _expert_kernel(tm, be_ref, nu_ref, xs_ref, wg_ref, wu_ref, wd_ref, ys_ref, h_ref, acc_ref):
    b = pl.program_id(0)
    j = pl.program_id(1)
    last = pl.num_programs(1) - 1
    used = b < nu_ref[0]

    @pl.when(used & (j == 0))
    def _():
        for s in range(TOK_TILES):
            h_ref[:, s * LANES:(s + 1) * LANES] = xs_ref[pl.ds(s, tm, stride=TOK_TILES), :].astype(BF16)
        acc_ref[...] = jnp.zeros_like(acc_ref)

    @pl.when(used)
    def _():
        _swiglu_step(h_ref[...], wg_ref, wu_ref, wd_ref, acc_ref)

    @pl.when(used & (j == last))
    def _():
        for s in range(TOK_TILES):
            ys_ref[pl.ds(s, tm, stride=TOK_TILES), :] = acc_ref[:, s * LANES:(s + 1) * LANES]

    @pl.when(jnp.logical_not(used) & (j == last))
    def _():
        ys_ref[...] = jnp.zeros_like(ys_ref)


def moe_experts(xs, blk_expert, n_used, w_gate, w_up, w_down):
    E, D, F = w_gate.shape
    tm = MOE_ROWS
    nb = xs.shape[0] // (tm * TOK_TILES)
    tf = _tile(F, FF_TILE)
    nj = F // tf
    jj = lambda b, j, be, nu: jnp.where(b < nu[0], j, nj - 1)
    grid_spec = pltpu.PrefetchScalarGridSpec(
        num_scalar_prefetch=2, grid=(nb, nj),
        in_specs=[pl.BlockSpec((tm * TOK_TILES, LANES), lambda b, j, be, nu: (b, 0)),
                  pl.BlockSpec((None, D, tf), lambda b, j, be, nu: (be[b], 0, jj(b, j, be, nu))),
                  pl.BlockSpec((None, D, tf), lambda b, j, be, nu: (be[b], 0, jj(b, j, be, nu))),
                  pl.BlockSpec((None, tf, D), lambda b, j, be, nu: (be[b], jj(b, j, be, nu), 0))],
        out_specs=pl.BlockSpec((tm * TOK_TILES, LANES), lambda b, j, be, nu: (b, 0)),
        scratch_shapes=[pltpu.VMEM((tm, D), BF16), pltpu.VMEM((tm, D), F32)])
    return pl.pallas_call(
        functools.partial(_expert_kernel, tm), grid_spec=grid_spec,
        out_shape=jax.ShapeDtypeStruct(xs.shape, F32),
        compiler_params=_params("arbitrary", "arbitrary"), name="moe_experts",
    )(blk_expert, n_used, xs, w_gate, w_up, w_down)


def _combine_kernel(tm, dest_ref, ys_ref, mf_ref, x_ref, g_ref, o_ref, buf_ref, y_ref, sem):
    def issue(t, carry):
        for k in range(TOP_K):
            pltpu.make_async_copy(ys_ref.at[pl.ds(dest_ref[k, t] * TOK_TILES, TOK_TILES), :],
                                  buf_ref.at[k, pl.ds(t * TOK_TILES, TOK_TILES), :], sem).start()
        return carry

    lax.fori_loop(0, tm, issue, 0)
    n = tm * TOK_TILES
    for k in range(TOP_K):
        pltpu.make_async_copy(ys_ref.at[pl.ds(0, n), :], buf_ref.at[k], sem).wait()
    mf = mf_ref[...]
    g1, g2 = mf[:, 0:1], mf[:, 1:2]
    for s in range(TOK_TILES):
        y_ref[:, s * LANES:(s + 1) * LANES] = (
            buf_ref[0, pl.ds(s, tm, stride=TOK_TILES), :] * g1 +
            buf_ref[1, pl.ds(s, tm, stride=TOK_TILES), :] * g2)
    o_ref[...] = _rmsnorm(x_ref[...] + y_ref[...], g_ref[...])


def moe_combine(ys, dest, meta_f, x, g_final, tm_pref=512):
    T, D = x.shape
    tm = _tile(T, tm_pref)
    return pl.pallas_call(
        functools.partial(_combine_kernel, tm), grid=(T // tm,),
        in_specs=[pl.BlockSpec((TOP_K, tm), lambda i: (0, i), memory_space=pltpu.SMEM),
                  pl.BlockSpec(memory_space=pl.ANY),
                  pl.BlockSpec((tm, LANES), lambda i: (i, 0)),
                  pl.BlockSpec((tm, D), lambda i: (i, 0)),
                  pl.BlockSpec((1, D), lambda i: (0, 0))],
        out_specs=pl.BlockSpec((tm, D), lambda i: (i, 0)),
        out_shape=jax.ShapeDtypeStruct((T, D), F32),
        scratch_shapes=[pltpu.VMEM((TOP_K, tm * TOK_TILES, LANES), F32), pltpu.VMEM((tm, D), F32),
                        pltpu.SemaphoreType.DMA(())],
        compiler_params=_params("arbitrary"), name="moe_combine",
    )(dest, ys, meta_f, x, g_final.reshape(1, D))


def moe_layer(x, g_ffn, g_final, w_router, w_gate, w_up, w_down):
    T = x.shape[0]
    tm = MOE_ROWS
    hs, meta_i, meta_f, cnt = moe_router(x, g_ffn, w_router)
    counts = cnt[0, :N_EXPERTS].astype(I32)
    padded = (counts + tm - 1) // tm * tm
    pad_end = jnp.cumsum(padded)
    pad_start = pad_end - padded
    idx = meta_i[:, 0:TOP_K].T
    rank = meta_i[:, TOP_K:2 * TOP_K].T
    dest = pad_start[idx] + rank
    nb = (T * TOP_K) // tm + N_EXPERTS
    n_used = (pad_end[-1] // tm).astype(I32)
    blk_row = jnp.minimum(jnp.arange(nb, dtype=I32), n_used - 1) * tm
    blk_expert = jnp.minimum(jnp.searchsorted(pad_end, blk_row, side='right'), N_EXPERTS - 1).astype(I32)
    xs = moe_dispatch(hs, dest, nb * tm)
    ys = moe_experts(xs, blk_expert, n_used.reshape(1), w_gate, w_up, w_down)
    return moe_combine(ys, dest, meta_f, x, g_final)


def _prep_weights(g_mix, g_mem, g_ffn, g_final, a_w_in, a_conv, a_log, a_dt_bias, a_o_norm,
                  a_mem_kv, a_w_out, b_w_in, b_sink, b_mem_kv, b_w_out, f_w_gate, f_w_up, f_w_down,
                  e_router, e_w_gate, e_w_up, e_w_down):
    s1 = DN_QKV
    s2 = s1 + DN_V
    s3 = s2 + 4 * DN_HEADS
    w = a_w_in[0]
    gates_w = jnp.zeros((D_MODEL, LANES), F32).at[:, :4 * DN_HEADS].set(w[:, s2:s3])
    wb = b_w_in[0]
    return dict(
        g_mix=g_mix, g_mem=g_mem, g_ffn=g_ffn, g_final=g_final,
        a_in=[w[:, :s1].astype(BF16), w[:, s1:s2].astype(BF16), w[:, s3:].astype(BF16), gates_w],
        a_conv=a_conv[0], a_log=a_log[0], a_dt_bias=a_dt_bias[0], a_o_norm=a_o_norm[0],
        a_mem_kv=a_mem_kv[0].astype(BF16), a_w_out=a_w_out[0].astype(BF16),
        b_in=[wb[:, :WA_Q].astype(BF16), wb[:, WA_Q:WA_Q + WA_KV].astype(BF16),
              wb[:, WA_Q + WA_KV:WA_Q + 2 * WA_KV].astype(BF16), wb[:, WA_Q + 2 * WA_KV:].astype(BF16)],
        b_sink=b_sink[0], b_mem_kv=b_mem_kv[0].astype(BF16), b_w_out=b_w_out[0].astype(BF16),
        f_w_gate=f_w_gate[0].astype(BF16), f_w_up=f_w_up[0].astype(BF16), f_w_down=f_w_down[0].astype(BF16),
        e_router=e_router[0], e_w_gate=e_w_gate[0].astype(BF16), e_w_up=e_w_up[0].astype(BF16),
        e_w_down=e_w_down[0].astype(BF16))


def _trunk(x, mem, p):
    B, L, D = x.shape
    M = mem.shape[1]
    T = B * L
    x2 = x.reshape(T, D)
    mem2 = mem.reshape(B * M, D)

    qkv, z, qc, gates = norm_proj(x2, p['g_mix'][0], p['a_in'], [BF16, BF16, BF16, F32])
    (kv,) = norm_proj(mem2, p['g_mem'][0], [p['a_mem_kv']], [BF16])
    q, k, v, pack, packt = dn_prep(qkv.reshape(B, L, DN_QKV), gates.reshape(B, L, LANES),
                                   p['a_conv'], p['a_log'], p['a_dt_bias'])
    o_f, o_b = delta_rule(q, k, v, pack, packt)
    xo = mem_attention(qc.reshape(B, L, MX_W), kv.reshape(B, M, 2 * MX_W))
    x1 = out_proj_a(o_f, o_b, z.reshape(B, L, DN_V), xo, x, p['a_o_norm'], p['a_w_out'])
    x1 = ffn(x1.reshape(T, D), p['g_ffn'][0], p['f_w_gate'], p['f_w_up'], p['f_w_down'])

    q, k, v, qc = norm_proj(x1, p['g_mix'][1], p['b_in'], [BF16, BF16, BF16, BF16])
    (kv,) = norm_proj(mem2, p['g_mem'][1], [p['b_mem_kv']], [BF16])
    cos, sin_signed = _rope_tables(L)
    o = window_attention(q.reshape(B, L, WA_Q), k.reshape(B, L, WA_KV), v.reshape(B, L, WA_KV),
                         p['b_sink'], cos, sin_signed)
    xo = mem_attention(qc.reshape(B, L, MX_W), kv.reshape(B, M, 2 * MX_W))
    x2 = out_proj_b(o, xo, x1.reshape(B, L, D), p['b_w_out'])
    y = moe_layer(x2.reshape(T, D), p['g_ffn'][1], p['g_final'], p['e_router'],
                  p['e_w_gate'], p['e_w_up'], p['e_w_down'])
    return y.reshape(B, L, D)


def kernel(x_prompt, x_sample, mem_prompt, mem_sample, g_mix, g_mem, g_ffn, g_final, a_w_in, a_conv,
           a_log, a_dt_bias, a_o_norm, a_mem_kv, a_w_out, b_w_in, b_sink, b_mem_kv, b_w_out,
           f_w_gate, f_w_up, f_w_down, e_router, e_w_gate, e_w_up, e_w_down):
    p = _prep_weights(g_mix, g_mem, g_ffn, g_final, a_w_in, a_conv, a_log, a_dt_bias, a_o_norm,
                      a_mem_kv, a_w_out, b_w_in, b_sink, b_mem_kv, b_w_out, f_w_gate, f_w_up, f_w_down,
                      e_router, e_w_gate, e_w_up, e_w_down)
    return (_trunk(x_prompt, mem_prompt, p), _trunk(x_sample, mem_sample, p))
```

```python
import functools
import math

import jax
import jax.numpy as jnp
from jax import lax
from jax.experimental import pallas as pl
from jax.experimental.pallas import tpu as pltpu

F32 = jnp.float32
BF16 = jnp.bfloat16
I32 = jnp.int32

EPS = 1e-6
D_MODEL = 1024
LANES = 128
SUBLANES = 8
TOK_TILES = D_MODEL // LANES
VMEM_LIMIT_BYTES = 56 * 1024 * 1024

DN_HEADS = 6
DN_D = 128
DN_CONV = 5
DN_CHUNK = 64
DN_QKV = 3 * DN_HEADS * DN_D
DN_V = DN_HEADS * DN_D
DN_TILE = 256
CONV_HALO = 16

WA_Q_HEADS = 12
WA_KV_HEADS = 4
WA_GROUP = WA_Q_HEADS // WA_KV_HEADS
WA_HEAD_DIM = 64
WA_BLOCK = 128
WA_QBLOCK = 256
WA_Q = WA_Q_HEADS * WA_HEAD_DIM
WA_KV = WA_KV_HEADS * WA_HEAD_DIM
ROPE_THETA = 10000.0

MX_HEADS = 4
MX_HEAD_DIM = 64
MX_W = MX_HEADS * MX_HEAD_DIM

D_FF = 3584
N_EXPERTS = 8
TOP_K = 2
FF_TILE = 512
MOE_ROWS_SMALL = 512
MOE_ROWS_LARGE = 1024


def _params(*sem):
    return pltpu.CompilerParams(dimension_semantics=sem, vmem_limit_bytes=VMEM_LIMIT_BYTES)


def _tile(n, pref):
    t = min(n, pref)
    assert n % t == 0, (n, pref)
    return t


def _dot(a, b):
    return jnp.dot(a, b, preferred_element_type=F32)


def _dot_nt(a, b):
    return lax.dot_general(a, b, (((1,), (1,)), ((), ())), preferred_element_type=F32)


def _dot_tn(a, b):
    return lax.dot_general(a, b, (((0,), (0,)), ((), ())), preferred_element_type=F32)


def _split3(x):
    hi = x.astype(BF16)
    r1 = x - hi.astype(F32)
    mid = r1.astype(BF16)
    lo = (r1 - mid.astype(F32)).astype(BF16)
    return hi, mid, lo


def _dot_x3(a, b):
    a_hi = a.astype(BF16)
    a_lo = (a - a_hi.astype(F32)).astype(BF16)
    b_hi = b.astype(BF16)
    b_lo = (b - b_hi.astype(F32)).astype(BF16)
    return _dot(a_hi, b_hi) + _dot(a_lo, b_hi) + _dot(a_hi, b_lo)


def _dot_mask_f32(mask_bf16, x):
    hi, mid, lo = _split3(x)
    return _dot(mask_bf16, hi) + _dot(mask_bf16, mid) + _dot(mask_bf16, lo)


def _rmsnorm(x, g):
    ms = jnp.mean(x * x, axis=-1, keepdims=True)
    return x * lax.rsqrt(ms + EPS) * g


def _silu(x):
    return x * jax.nn.sigmoid(x)


def _rope(x, cos, sin_signed):
    n, w = x.shape
    lane = lax.broadcasted_iota(I32, (n, LANES), 1)
    first_half = (lane % WA_HEAD_DIM) < (WA_HEAD_DIM // 2)
    outs = []
    for gidx in range(w // LANES):
        xs = x[:, gidx * LANES:(gidx + 1) * LANES]
        rot = jnp.where(first_half, pltpu.roll(xs, LANES - WA_HEAD_DIM // 2, 1),
                        pltpu.roll(xs, WA_HEAD_DIM // 2, 1))
        outs.append(xs * cos + rot * sin_signed)
    return jnp.concatenate(outs, axis=1)


def _norm_proj_kernel(n_out, rope_scales, x_ref, g_ref, *refs):
    if rope_scales is not None:
        cos_ref, sin_ref, refs = refs[0], refs[1], refs[2:]
    w_refs, o_refs = refs[:n_out], refs[n_out:]
    hf = _rmsnorm(x_ref[...], g_ref[...])
    hb = hf.astype(BF16)
    for idx, (w_ref, o_ref) in enumerate(zip(w_refs, o_refs)):
        if w_ref.dtype == F32:
            y = _dot_x3(hf, w_ref[...])
        else:
            y = _dot(hb, w_ref[...])
        if rope_scales is not None and rope_scales[idx] is not None:
            y = _rope(y, cos_ref[...], sin_ref[...]) * rope_scales[idx]
        o_ref[...] = y.astype(o_ref.dtype)


def norm_proj(x, g, weights, out_dtypes, rope=None, tm_pref=512):
    T, D = x.shape
    tm = _tile(T, tm_pref)
    n_out = len(weights)
    in_specs = [pl.BlockSpec((tm, D), lambda i: (i, 0)), pl.BlockSpec((1, D), lambda i: (0, 0))]
    args = [x, g.reshape(1, D)]
    scales = None
    if rope is not None:
        cos, sin_signed, scales = rope
        tm = _tile(cos.shape[0], tm)
        per_seq = cos.shape[0] // tm
        in_specs = [pl.BlockSpec((tm, D), lambda i: (i, 0)), pl.BlockSpec((1, D), lambda i: (0, 0))]
        in_specs += [pl.BlockSpec((tm, LANES), lambda i: (i % per_seq, 0))] * 2
        args += [cos, sin_signed]
    in_specs += [pl.BlockSpec(w.shape, lambda i: (0, 0)) for w in weights]
    out_specs = [pl.BlockSpec((tm, w.shape[1]), lambda i: (i, 0)) for w in weights]
    out_shape = [jax.ShapeDtypeStruct((T, w.shape[1]), dt) for w, dt in zip(weights, out_dtypes)]
    return pl.pallas_call(
        functools.partial(_norm_proj_kernel, n_out, scales),
        grid=(T // tm,), in_specs=in_specs, out_specs=out_specs, out_shape=out_shape,
        compiler_params=_params("parallel"), name="norm_proj",
    )(*args, *weights)


def _dn_prep_kernel(tl, cur_ref, prev_ref, next_ref, gates_ref, cw_ref, alog_ref, dtb_ref,
                    q_ref, k_ref, v_ref, pack_ref, packt_ref, ext_ref):
    i = pl.program_id(1)
    nt = pl.num_programs(1)
    has_prev = (i > 0).astype(F32)
    has_next = (i < nt - 1).astype(F32)
    ext_ref[0:CONV_HALO, :] = prev_ref[...].astype(F32) * has_prev
    ext_ref[CONV_HALO:CONV_HALO + tl, :] = cur_ref[...].astype(F32)
    ext_ref[CONV_HALO + tl:, :] = next_ref[...].astype(F32) * has_next
    pad = (DN_CONV - 1) // 2
    for cg in range(DN_QKV // LANES):
        sl = slice(cg * LANES, (cg + 1) * LANES)
        acc = None
        for j in range(DN_CONV):
            term = ext_ref[pl.ds(CONV_HALO - pad + j, tl), sl] * cw_ref[j:j + 1, sl]
            acc = term if acc is None else acc + term
        y = _silu(acc)
        if cg < DN_HEADS:
            y = y * lax.rsqrt(jnp.sum(y * y, axis=-1, keepdims=True) + EPS) * (DN_D ** -0.5)
            q_ref[cg] = y.astype(q_ref.dtype)
        elif cg < 2 * DN_HEADS:
            y = y * lax.rsqrt(jnp.sum(y * y, axis=-1, keepdims=True) + EPS)
            k_ref[cg - DN_HEADS] = y.astype(k_ref.dtype)
        else:
            v_ref[cg - 2 * DN_HEADS] = y.astype(v_ref.dtype)

    gt = gates_ref[...]
    lane = lax.broadcasted_iota(I32, gt.shape, 1)
    beta = jax.nn.sigmoid(gt)
    xg = gt + dtb_ref[...]
    softplus = jnp.maximum(xg, 0.0) + jnp.log1p(jnp.exp(-jnp.abs(xg)))
    g = -jnp.exp(alog_ref[...]) * softplus
    g = jnp.where((lane >= 2 * DN_HEADS) & (lane < 4 * DN_HEADS), g, 0.0)
    r = lax.broadcasted_iota(I32, (tl, tl), 0)
    c = lax.broadcasted_iota(I32, (tl, tl), 1)
    same = (r // DN_CHUNK) == (c // DN_CHUNK)
    m_fwd = (same & (r >= c)).astype(BF16)
    m_bwd = (same & (r <= c)).astype(BF16)
    m_all = same.astype(BF16)
    hi, mid, lo = _split3(g)
    g_fwd = _dot(m_fwd, hi) + _dot(m_fwd, mid) + _dot(m_fwd, lo)
    g_bwd = _dot(m_bwd, hi) + _dot(m_bwd, mid) + _dot(m_bwd, lo)
    g_tot = pltpu.roll(_dot(m_all, hi) + _dot(m_all, mid) + _dot(m_all, lo), 2 * DN_HEADS, 1)
    pack = jnp.where(lane < 2 * DN_HEADS, beta,
                     jnp.where(lane < 3 * DN_HEADS, g_fwd,
                               jnp.where(lane < 4 * DN_HEADS, g_bwd,
                                         jnp.where(lane < 6 * DN_HEADS, g_tot, 0.0))))
    pack_ref[...] = pack
    packt_ref[...] = pack.T


def dn_prep(qkv, gates, conv_w, a_log, dt_bias):
    B, L, _ = qkv.shape
    tl = _tile(L, DN_TILE)
    nt = L // tl
    hb = tl // CONV_HALO
    nhalo = L // CONV_HALO
    cw = jnp.zeros((SUBLANES, DN_QKV), F32).at[:DN_CONV].set(conv_w)
    alog = jnp.zeros((1, LANES), F32).at[0, 2 * DN_HEADS:4 * DN_HEADS].set(a_log.reshape(-1))
    dtb = jnp.zeros((1, LANES), F32).at[0, 2 * DN_HEADS:4 * DN_HEADS].set(dt_bias.reshape(-1))
    hm = jax.ShapeDtypeStruct((B, DN_HEADS, L, DN_D), BF16)
    hm_spec = pl.BlockSpec((None, DN_HEADS, tl, DN_D), lambda b, i: (b, 0, i, 0))
    return pl.pallas_call(
        functools.partial(_dn_prep_kernel, tl),
        grid=(B, nt),
        in_specs=[
            pl.BlockSpec((None, tl, DN_QKV), lambda b, i: (b, i, 0)),
            pl.BlockSpec((None, CONV_HALO, DN_QKV), lambda b, i: (b, jnp.maximum(i * hb - 1, 0), 0)),
            pl.BlockSpec((None, CONV_HALO, DN_QKV), lambda b, i: (b, jnp.minimum((i + 1) * hb, nhalo - 1), 0)),
            pl.BlockSpec((None, tl, LANES), lambda b, i: (b, i, 0)),
            pl.BlockSpec((SUBLANES, DN_QKV), lambda b, i: (0, 0)),
            pl.BlockSpec((1, LANES), lambda b, i: (0, 0)),
            pl.BlockSpec((1, LANES), lambda b, i: (0, 0)),
        ],
        out_specs=[hm_spec, hm_spec, hm_spec,
                   pl.BlockSpec((None, tl, LANES), lambda b, i: (b, i, 0)),
                   pl.BlockSpec((None, LANES, tl), lambda b, i: (b, 0, i))],
        out_shape=[hm, hm, hm, jax.ShapeDtypeStruct((B, L, LANES), F32),
                   jax.ShapeDtypeStruct((B, LANES, L), F32)],
        scratch_shapes=[pltpu.VMEM((tl + 2 * CONV_HALO, DN_QKV), F32)],
        compiler_params=_params("parallel", "parallel"), name="dn_prep",
    )(qkv, qkv, qkv, gates, cw, alog, dtb)


def _dn_tile_stage(units, tl, refs_f, refs_b, masks_f, masks_b, u_ref, w_ref, qd_ref, kd_ref, aqk_ref):
    n = len(units)
    nchunk = tl // DN_CHUNK
    refs = [refs_f if d == 0 else refs_b for d, _ in units]
    masks = [masks_f if d == 0 else masks_b for d, _ in units]
    col = []
    for (d, h), (q_ref, k_ref, v_ref, pack_ref, packt_ref) in zip(units, refs):
        pk = pack_ref[...]
        lane = lax.broadcasted_iota(I32, pk.shape, 1)
        pick = lambda li: jnp.sum(jnp.where(lane == li, pk, 0.0), axis=1, keepdims=True)
        li_g = (2 + d) * DN_HEADS + h
        col.append((pick(d * DN_HEADS + h), pick(li_g), pick((4 + d) * DN_HEADS + h),
                    packt_ref[pl.ds(li_g, 1), :]))
    q16 = [r[0][h] for (d, h), r in zip(units, refs)]
    k16 = [r[1][h] for (d, h), r in zip(units, refs)]
    kf = [x.astype(F32) for x in k16]
    kb = [kf[i] * col[i][0] for i in range(n)]
    decay = [jnp.exp(jnp.where(masks[i][0], col[i][1] - col[i][3], -jnp.inf)) for i in range(n)]
    kk = [_dot_nt(kb[i].astype(BF16), k16[i]) for i in range(n)]
    qk = [_dot_nt(q16[i], k16[i]) for i in range(n)]
    a = [jnp.where(masks[i][1], kk[i] * decay[i], 0.0) for i in range(n)]
    for i, (d, h) in enumerate(units):
        a_qk = (qk[i] * decay[i]).astype(BF16)
        for c in range(nchunk):
            rs = slice(c * DN_CHUNK, (c + 1) * DN_CHUNK)
            aqk_ref[d * DN_HEADS + h, rs, :] = a_qk[rs, rs]
    t_inv = [masks[i][2] - a[i] for i in range(n)]
    p16 = [x.astype(BF16) for x in a]
    for _ in range(int(math.log2(DN_CHUNK)) - 1):
        p16 = [_dot(x, x).astype(BF16) for x in p16]
        t_inv = [t_inv[i] + _dot(t_inv[i].astype(BF16), p16[i]) for i in range(n)]
    e_g = [jnp.exp(col[i][1]) for i in range(n)]
    rhs = [jnp.concatenate([refs[i][2][units[i][1]].astype(F32) * col[i][0], kb[i] * e_g[i]],
                           axis=1).astype(BF16) for i in range(n)]
    uw = [_dot(t_inv[i].astype(BF16), rhs[i]) for i in range(n)]
    for i, (d, h) in enumerate(units):
        ui = d * DN_HEADS + h
        u_ref[ui] = uw[i][:, :DN_D]
        w_ref[ui] = uw[i][:, DN_D:].astype(BF16)
        qd_ref[ui] = (q16[i].astype(F32) * e_g[i]).astype(BF16)
        kd_ref[ui] = (kf[i] * jnp.exp(col[i][2] - col[i][1])).astype(BF16)


def _dn_scan_stage(tl, ptf_ref, ptb_ref, of_ref, ob_ref, s_ref, u_ref, w_ref, qd_ref, kd_ref, aqk_ref):
    nchunk = tl // DN_CHUNK
    chains = [(d, h) for h in range(DN_HEADS) for d in range(2)]
    ids = [d * DN_HEADS + h for d, h in chains]
    for step in range(nchunk):
        rows = []
        for d, h in chains:
            c = step if d == 0 else nchunk - 1 - step
            rows.append(slice(c * DN_CHUNK, (c + 1) * DN_CHUNK))
        s = [s_ref[ui] for ui in ids]
        sb = [x.astype(BF16) for x in s]
        v_new = [(u_ref[ui, rs, :] - _dot(w_ref[ui, rs, :], b)).astype(BF16)
                 for ui, rs, b in zip(ids, rows, sb)]
        o = [_dot(qd_ref[ui, rs, :], b) + _dot(aqk_ref[ui, rs, :], vn)
             for ui, rs, b, vn in zip(ids, rows, sb, v_new)]
        for (d, h), ui, rs, x, vn, oo in zip(chains, ids, rows, s, v_new, o):
            pt_ref = ptf_ref if d == 0 else ptb_ref
            li_tot = (4 + d) * DN_HEADS + h
            tot = pt_ref[li_tot:li_tot + 1, rs.start:rs.start + 1]
            s_ref[ui] = x * jnp.exp(tot) + _dot_tn(kd_ref[ui, rs, :], vn)
            (of_ref if d == 0 else ob_ref)[h, rs, :] = oo.astype(of_ref.dtype)


def _dn_kernel(tl, qf_ref, kf_ref, vf_ref, pf_ref, ptf_ref, qb_ref, kb_ref, vb_ref, pb_ref, ptb_ref,
               of_ref, ob_ref, s_ref, u_ref, w_ref, qd_ref, kd_ref, aqk_ref):
    @pl.when(pl.program_id(1) == 0)
    def _():
        s_ref[...] = jnp.zeros_like(s_ref)

    r = lax.broadcasted_iota(I32, (tl, tl), 0)
    c = lax.broadcasted_iota(I32, (tl, tl), 1)
    same = (r // DN_CHUNK) == (c // DN_CHUNK)
    eye = (r == c).astype(F32)
    masks_f = (same & (r >= c), same & (r > c), eye)
    masks_b = (same & (r <= c), same & (r < c), eye)
    refs_f = (qf_ref, kf_ref, vf_ref, pf_ref, ptf_ref)
    refs_b = (qb_ref, kb_ref, vb_ref, pb_ref, ptb_ref)

    def head_pair(hp, carry):
        units = [(0, 2 * hp), (1, 2 * hp), (0, 2 * hp + 1), (1, 2 * hp + 1)]
        _dn_tile_stage(units, tl, refs_f, refs_b, masks_f, masks_b, u_ref, w_ref, qd_ref, kd_ref, aqk_ref)
        return carry

    lax.fori_loop(0, DN_HEADS // 2, head_pair, 0)
    _dn_scan_stage(tl, ptf_ref, ptb_ref, of_ref, ob_ref, s_ref, u_ref, w_ref, qd_ref, kd_ref, aqk_ref)


def delta_rule(q, k, v, pack, packt):
    B, H, L, _ = q.shape
    tl = _tile(L, DN_TILE)
    nt = L // tl
    fwd = lambda b, i: (b, 0, i, 0)
    bwd = lambda b, i: (b, 0, nt - 1 - i, 0)
    hm_f = pl.BlockSpec((None, H, tl, DN_D), fwd)
    hm_b = pl.BlockSpec((None, H, tl, DN_D), bwd)
    pk_f = pl.BlockSpec((None, tl, LANES), lambda b, i: (b, i, 0))
    pk_b = pl.BlockSpec((None, tl, LANES), lambda b, i: (b, nt - 1 - i, 0))
    pt_f = pl.BlockSpec((None, LANES, tl), lambda b, i: (b, 0, i))
    pt_b = pl.BlockSpec((None, LANES, tl), lambda b, i: (b, 0, nt - 1 - i))
    o = jax.ShapeDtypeStruct((B, H, L, DN_D), BF16)
    return pl.pallas_call(
        functools.partial(_dn_kernel, tl),
        grid=(B, nt),
        in_specs=[hm_f, hm_f, hm_f, pk_f, pt_f, hm_b, hm_b, hm_b, pk_b, pt_b],
        out_specs=[hm_f, hm_b], out_shape=[o, o],
        scratch_shapes=[pltpu.VMEM((2 * H, DN_D, DN_D), F32),
                        pltpu.VMEM((2 * H, tl, DN_D), F32),
                        pltpu.VMEM((2 * H, tl, DN_D), BF16),
                        pltpu.VMEM((2 * H, tl, DN_D), BF16),
                        pltpu.VMEM((2 * H, tl, DN_D), BF16),
                        pltpu.VMEM((2 * H, tl, DN_CHUNK), BF16)],
        compiler_params=_params("parallel", "arbitrary"), name="delta_rule",
    )(q, k, v, pack, packt, q, k, v, pack, packt)


def _mem_attn_kernel(q_ref, kv_ref, o_ref):
    q = q_ref[...]
    kv = kv_ref[...]
    outs = []
    for h in range(MX_HEADS):
        hs = slice(h * MX_HEAD_DIM, (h + 1) * MX_HEAD_DIM)
        vs = slice(MX_W + h * MX_HEAD_DIM, MX_W + (h + 1) * MX_HEAD_DIM)
        s = _dot_nt(q[:, hs], kv[:, hs]) * (MX_HEAD_DIM ** -0.5)
        m = jnp.max(s, axis=-1, keepdims=True)
        p = jnp.exp(s - m)
        den = jnp.sum(p, axis=-1, keepdims=True)
        outs.append(_dot(p.astype(BF16), kv[:, vs]) / den)
    o_ref[...] = jnp.concatenate(outs, axis=1).astype(o_ref.dtype)


def mem_attention(qc, kv, tl_pref=512):
    B, L, _ = qc.shape
    M = kv.shape[1]
    tl = _tile(L, tl_pref)
    return pl.pallas_call(
        _mem_attn_kernel, grid=(B, L // tl),
        in_specs=[pl.BlockSpec((None, tl, MX_W), lambda b, i: (b, i, 0)),
                  pl.BlockSpec((None, M, 2 * MX_W), lambda b, i: (b, 0, 0))],
        out_specs=pl.BlockSpec((None, tl, MX_W), lambda b, i: (b, i, 0)),
        out_shape=jax.ShapeDtypeStruct((B, L, MX_W), BF16),
        compiler_params=_params("parallel", "parallel"), name="mem_attention",
    )(qc, kv)


def _out_proj_a_kernel(of_ref, ob_ref, z_ref, xo_ref, x_ref, on_ref, w_ref, o_ref):
    parts = []
    z = z_ref[...].astype(F32)
    for h in range(DN_HEADS):
        o = of_ref[h].astype(F32) + ob_ref[h].astype(F32)
        y = _rmsnorm(o, on_ref[...]) * _silu(z[:, h * DN_D:(h + 1) * DN_D])
        parts.append(y.astype(BF16))
    parts.append(xo_ref[...])
    mixed = jnp.concatenate(parts, axis=1)
    o_ref[...] = x_ref[...] + _dot(mixed, w_ref[...])


def out_proj_a(o_f, o_b, z, xo, x, o_norm, w_out, tm_pref=512):
    B, H, L, _ = o_f.shape
    D = x.shape[-1]
    tm = _tile(L, tm_pref)
    hm = pl.BlockSpec((None, H, tm, DN_D), lambda b, i: (b, 0, i, 0))
    row = lambda w: pl.BlockSpec((None, tm, w), lambda b, i: (b, i, 0))
    return pl.pallas_call(
        _out_proj_a_kernel, grid=(B, L // tm),
        in_specs=[hm, hm, row(DN_V), row(MX_W), row(D),
                  pl.BlockSpec((1, DN_D), lambda b, i: (0, 0)),
                  pl.BlockSpec(w_out.shape, lambda b, i: (0, 0))],
        out_specs=row(D), out_shape=jax.ShapeDtypeStruct((B, L, D), F32),
        compiler_params=_params("parallel", "parallel"), name="out_proj_a",
    )(o_f, o_b, z, xo, x, o_norm.reshape(1, DN_D), w_out)


def _out_proj_b_kernel(o_ref, xo_ref, x_ref, w_ref, out_ref):
    mixed = jnp.concatenate([o_ref[...], xo_ref[...]], axis=1)
    out_ref[...] = x_ref[...] + _dot(mixed, w_ref[...])


def out_proj_b(o, xo, x, w_out, tm_pref=512):
    B, L, D = x.shape
    tm = _tile(L, tm_pref)
    row = lambda w: pl.BlockSpec((None, tm, w), lambda b, i: (b, i, 0))
    return pl.pallas_call(
        _out_proj_b_kernel, grid=(B, L // tm),
        in_specs=[row(WA_Q), row(MX_W), row(D), pl.BlockSpec(w_out.shape, lambda b, i: (0, 0))],
        out_specs=row(D), out_shape=jax.ShapeDtypeStruct((B, L, D), F32),
        compiler_params=_params("parallel", "parallel"), name="out_proj_b",
    )(o, xo, x, w_out)


def _swiglu_step(h, wg_ref, wu_ref, wd_ref, acc_ref):
    gate = _dot(h, wg_ref[...])
    up = _dot(h, wu_ref[...])
    acc_ref[...] += _dot((_silu(gate) * up).astype(BF16), wd_ref[...])


def _ffn_kernel(x_ref, g_ref, wg_ref, wu_ref, wd_ref, o_ref, h_ref, acc_ref):
    j = pl.program_id(1)

    @pl.when(j == 0)
    def _():
        h_ref[...] = _rmsnorm(x_ref[...], g_ref[...]).astype(BF16)
        acc_ref[...] = jnp.zeros_like(acc_ref)

    _swiglu_step(h_ref[...], wg_ref, wu_ref, wd_ref, acc_ref)

    @pl.when(j == pl.num_programs(1) - 1)
    def _():
        o_ref[...] = x_ref[...] + acc_ref[...]


def ffn(x, g, w_gate, w_up, w_down, tm_pref=1024):
    T, D = x.shape
    F = w_gate.shape[1]
    tm = _tile(T, tm_pref)
    tf = _tile(F, FF_TILE)
    return pl.pallas_call(
        _ffn_kernel, grid=(T // tm, F // tf),
        in_specs=[pl.BlockSpec((tm, D), lambda i, j: (i, 0)),
                  pl.BlockSpec((1, D), lambda i, j: (0, 0)),
                  pl.BlockSpec((D, tf), lambda i, j: (0, j)),
                  pl.BlockSpec((D, tf), lambda i, j: (0, j)),
                  pl.BlockSpec((tf, D), lambda i, j: (j, 0))],
        out_specs=pl.BlockSpec((tm, D), lambda i, j: (i, 0)),
        out_shape=jax.ShapeDtypeStruct((T, D), F32),
        scratch_shapes=[pltpu.VMEM((tm, D), BF16), pltpu.VMEM((tm, D), F32)],
        compiler_params=_params("parallel", "arbitrary"), name="ffn",
    )(x, g.reshape(1, D), w_gate, w_up, w_down)


def _win_attn_kernel(sink_ref, q_ref, kp_ref, kc_ref, kn_ref, vp_ref, vc_ref, vn_ref, bias_ref, o_ref):
    i = pl.program_id(1)
    n = pl.num_programs(1)
    qb = q_ref.shape[0]
    q = q_ref[...]
    k = jnp.concatenate([kp_ref[...], kc_ref[...], kn_ref[...]], axis=0)
    v = jnp.concatenate([vp_ref[...], vc_ref[...], vn_ref[...]], axis=0)
    nk = k.shape[0]
    bias = bias_ref[(i == 0).astype(I32) + 2 * (i == n - 1).astype(I32)]
    rows = WA_GROUP * qb
    grp = lax.broadcasted_iota(I32, (rows, 1), 0) // qb
    outs = [None] * WA_Q_HEADS
    for h in range(WA_KV_HEADS):
        hs = slice(h * WA_HEAD_DIM, (h + 1) * WA_HEAD_DIM)
        qs = jnp.concatenate(
            [q[:, (h * WA_GROUP + g) * WA_HEAD_DIM:(h * WA_GROUP + g + 1) * WA_HEAD_DIM]
             for g in range(WA_GROUP)], axis=0)
        sk = jnp.zeros((rows, 1), F32)
        for g in range(WA_GROUP):
            sk = jnp.where(grp == g, sink_ref[h * WA_GROUP + g], sk)
        s = (_dot_nt(qs, k[:, hs]).reshape(WA_GROUP, qb, nk) + bias[None]).reshape(rows, nk)
        m = jnp.maximum(jnp.max(s, axis=-1, keepdims=True), sk)
        p = jnp.exp(s - m)
        den = jnp.sum(p, axis=-1, keepdims=True) + jnp.exp(sk - m)
        o = _dot(p.astype(BF16), v[:, hs]) / den
        for g in range(WA_GROUP):
            outs[h * WA_GROUP + g] = o[g * qb:(g + 1) * qb]
    o_ref[...] = jnp.concatenate(outs, axis=1).astype(o_ref.dtype)


def _window_bias(qb):
    nk = qb + 2 * WA_BLOCK
    qi = jnp.arange(qb)[:, None]
    kj = jnp.arange(nk)[None, :]
    band = jnp.abs(kj - WA_BLOCK - qi) <= WA_BLOCK
    not_prev = kj >= WA_BLOCK
    not_next = kj < WA_BLOCK + qb
    variants = [band, band & not_prev, band & not_next, band & not_prev & not_next]
    return jnp.stack([jnp.where(m, 0.0, -jnp.inf).astype(F32) for m in variants])


def window_attention(q, k, v, sink):
    B, L, _ = q.shape
    qb = _tile(L, WA_QBLOCK)
    n = L // qb
    per = qb // WA_BLOCK
    nhalo = L // WA_BLOCK
    prev = lambda b, i, s: (b, jnp.maximum(i * per - 1, 0), 0)
    cur = lambda b, i, s: (b, i, 0)
    nxt = lambda b, i, s: (b, jnp.minimum((i + 1) * per, nhalo - 1), 0)
    halo = lambda f: pl.BlockSpec((None, WA_BLOCK, WA_KV), f)
    body = pl.BlockSpec((None, qb, WA_KV), cur)
    bias = _window_bias(qb)
    grid_spec = pltpu.PrefetchScalarGridSpec(
        num_scalar_prefetch=1, grid=(B, n),
        in_specs=[pl.BlockSpec((None, qb, WA_Q), cur),
                  halo(prev), body, halo(nxt), halo(prev), body, halo(nxt),
                  pl.BlockSpec(bias.shape, lambda b, i, s: (0, 0, 0))],
        out_specs=pl.BlockSpec((None, qb, WA_Q), cur))
    return pl.pallas_call(
        _win_attn_kernel, grid_spec=grid_spec,
        out_shape=jax.ShapeDtypeStruct((B, L, WA_Q), BF16),
        compiler_params=_params("parallel", "parallel"), name="window_attention",
    )(sink, q, k, k, k, v, v, v, bias)


def _rope_tables(L):
    inv = ROPE_THETA ** (-jnp.arange(0, WA_HEAD_DIM, 2, dtype=F32) / WA_HEAD_DIM)
    ang = jnp.arange(L, dtype=F32)[:, None] * inv[None, :]
    cos, sin = jnp.cos(ang), jnp.sin(ang)
    reps = LANES // WA_HEAD_DIM
    return jnp.tile(jnp.concatenate([cos, cos], axis=1), (1, reps)), \
        jnp.tile(jnp.concatenate([-sin, sin], axis=1), (1, reps))


def _router_kernel(tm, x_ref, g_ref, wr_ref, hs_ref, mi_ref, mf_ref, cnt_ref, carry_ref):
    @pl.when(pl.program_id(0) == 0)
    def _():
        carry_ref[...] = jnp.zeros_like(carry_ref)

    hf = _rmsnorm(x_ref[...], g_ref[...])
    for s in range(TOK_TILES):
        hs_ref[pl.ds(s, tm, stride=TOK_TILES), :] = hf[:, s * LANES:(s + 1) * LANES]
    logits = _dot_x3(hf, wr_ref[...])
    lane = lax.broadcasted_iota(I32, logits.shape, 1)
    lg = jnp.where(lane < N_EXPERTS, logits, -jnp.inf)
    m1 = jnp.max(lg, axis=1, keepdims=True)
    i1 = jnp.min(jnp.where(lg == m1, lane, LANES), axis=1, keepdims=True)
    lg2 = jnp.where(lane == i1, -jnp.inf, lg)
    m2 = jnp.max(lg2, axis=1, keepdims=True)
    i2 = jnp.min(jnp.where(lg2 == m2, lane, LANES), axis=1, keepdims=True)
    e2 = jnp.exp(m2 - m1)
    g1 = 1.0 / (1.0 + e2)
    g2 = e2 / (1.0 + e2)
    oh1 = lane == i1
    oh2 = lane == i2
    oh = (oh1 | oh2).astype(F32)
    r = lax.broadcasted_iota(I32, (tm, tm), 0)
    c = lax.broadcasted_iota(I32, (tm, tm), 1)
    before = _dot((r > c).astype(BF16), oh.astype(BF16)) + carry_ref[...]
    rank1 = jnp.sum(jnp.where(oh1, before, 0.0), axis=1, keepdims=True).astype(I32)
    rank2 = jnp.sum(jnp.where(oh2, before, 0.0), axis=1, keepdims=True).astype(I32)
    carry_ref[...] += jnp.sum(oh, axis=0, keepdims=True)
    mi_ref[...] = jnp.where(lane == 0, i1, jnp.where(lane == 1, i2, jnp.where(lane == 2, rank1,
                            jnp.where(lane == 3, rank2, 0))))
    mf_ref[...] = jnp.where(lane == 0, g1, jnp.where(lane == 1, g2, 0.0))
    cnt_ref[...] = jnp.broadcast_to(carry_ref[...], cnt_ref.shape)


def moe_router(x, g, w_router, tm_pref=512):
    T, D = x.shape
    tm = _tile(T, tm_pref)
    wr = jnp.zeros((D, LANES), F32).at[:, :N_EXPERTS].set(w_router)
    return pl.pallas_call(
        functools.partial(_router_kernel, tm), grid=(T // tm,),
        in_specs=[pl.BlockSpec((tm, D), lambda i: (i, 0)),
                  pl.BlockSpec((1, D), lambda i: (0, 0)),
                  pl.BlockSpec((D, LANES), lambda i: (0, 0))],
        out_specs=[pl.BlockSpec((tm * TOK_TILES, LANES), lambda i: (i, 0)),
                   pl.BlockSpec((tm, LANES), lambda i: (i, 0)),
                   pl.BlockSpec((tm, LANES), lambda i: (i, 0)),
                   pl.BlockSpec((SUBLANES, LANES), lambda i: (0, 0))],
        out_shape=[jax.ShapeDtypeStruct((T * TOK_TILES, LANES), F32),
                   jax.ShapeDtypeStruct((T, LANES), I32),
                   jax.ShapeDtypeStruct((T, LANES), F32),
                   jax.ShapeDtypeStruct((SUBLANES, LANES), F32)],
        scratch_shapes=[pltpu.VMEM((1, LANES), F32)],
        compiler_params=_params("arbitrary"), name="moe_router",
    )(x, g.reshape(1, D), wr)


def _row_copy(src, src_row, dst, dst_row, sem):
    return pltpu.make_async_copy(src.at[pl.ds(src_row * TOK_TILES, TOK_TILES), :],
                                 dst.at[pl.ds(dst_row * TOK_TILES, TOK_TILES), :], sem)


def _dispatch_kernel(tm, dest_ref, hs_ref, xs_in_ref, xs_ref, sem):
    del xs_in_ref

    def issue(t, carry):
        for k in range(TOP_K):
            _row_copy(hs_ref, t, xs_ref, dest_ref[k, t], sem).start()
        return carry

    lax.fori_loop(0, tm, issue, 0, unroll=8)
    n = tm * TOK_TILES
    for k in range(TOP_K):
        pltpu.make_async_copy(hs_ref, xs_ref.at[pl.ds(0, n), :], sem).wait()


def moe_dispatch(hs, dest, n_rows, tm_pref=512):
    T = dest.shape[1]
    tm = _tile(T, tm_pref)
    xs0 = jnp.zeros((n_rows * TOK_TILES, LANES), F32)
    return pl.pallas_call(
        functools.partial(_dispatch_kernel, tm), grid=(T // tm,),
        in_specs=[pl.BlockSpec((TOP_K, tm), lambda i: (0, i), memory_space=pltpu.SMEM),
                  pl.BlockSpec((tm * TOK_TILES, LANES), lambda i: (i, 0)),
                  pl.BlockSpec(memory_space=pl.ANY)],
        out_specs=pl.BlockSpec(memory_space=pl.ANY),
        out_shape=jax.ShapeDtypeStruct(xs0.shape, F32),
        scratch_shapes=[pltpu.SemaphoreType.DMA(())],
        input_output_aliases={2: 0},
        compiler_params=_params("arbitrary"), name="moe_dispatch",
    )(dest, hs, xs0)


def _expert_kernel(tm, be_ref, nu_ref, xs_ref, wg_ref, wu_ref, wd_ref, ys_ref, h_ref, acc_ref):
    b = pl.program_id(0)
    j = pl.program_id(1)
    last = pl.num_programs(1) - 1
    used = b < nu_ref[0]

    @pl.when(used & (j == 0))
    def _():
        for s in range(TOK_TILES):
            h_ref[:, s * LANES:(s + 1) * LANES] = xs_ref[pl.ds(s, tm, stride=TOK_TILES), :].astype(BF16)
        acc_ref[...] = jnp.zeros_like(acc_ref)

    @pl.when(used)
    def _():
        _swiglu_step(h_ref[...], wg_ref, wu_ref, wd_ref, acc_ref)

    @pl.when(used & (j == last))
    def _():
        for s in range(TOK_TILES):
            ys_ref[pl.ds(s, tm, stride=TOK_TILES), :] = acc_ref[:, s * LANES:(s + 1) * LANES]

    @pl.when(jnp.logical_not(used) & (j == last))
    def _():
        ys_ref[...] = jnp.zeros_like(ys_ref)


def moe_experts(xs, blk_expert, n_used, w_gate, w_up, w_down, tm):
    E, D, F = w_gate.shape
    nb = xs.shape[0] // (tm * TOK_TILES)
    tf = _tile(F, FF_TILE)
    nj = F // tf
    jj = lambda b, j, be, nu: jnp.where(b < nu[0], j, nj - 1)
    grid_spec = pltpu.PrefetchScalarGridSpec(
        num_scalar_prefetch=2, grid=(nb, nj),
        in_specs=[pl.BlockSpec((tm * TOK_TILES, LANES), lambda b, j, be, nu: (b, 0)),
                  pl.BlockSpec((None, D, tf), lambda b, j, be, nu: (be[b], 0, jj(b, j, be, nu))),
                  pl.BlockSpec((None, D, tf), lambda b, j, be, nu: (be[b], 0, jj(b, j, be, nu))),
                  pl.BlockSpec((None, tf, D), lambda b, j, be, nu: (be[b], jj(b, j, be, nu), 0))],
        out_specs=pl.BlockSpec((tm * TOK_TILES, LANES), lambda b, j, be, nu: (b, 0)),
        scratch_shapes=[pltpu.VMEM((tm, D), BF16), pltpu.VMEM((tm, D), F32)])
    return pl.pallas_call(
        functools.partial(_expert_kernel, tm), grid_spec=grid_spec,
        out_shape=jax.ShapeDtypeStruct(xs.shape, F32),
        compiler_params=_params("arbitrary", "arbitrary"), name="moe_experts",
    )(blk_expert, n_used, xs, w_gate, w_up, w_down)


def _combine_kernel(tm, dest_ref, ys_ref, mf_ref, x_ref, g_ref, o_ref, buf_ref, y_ref, sem):
    def issue(t, carry):
        for k in range(TOP_K):
            pltpu.make_async_copy(ys_ref.at[pl.ds(dest_ref[k, t] * TOK_TILES, TOK_TILES), :],
                                  buf_ref.at[k, pl.ds(t * TOK_TILES, TOK_TILES), :], sem).start()
        return carry

    lax.fori_loop(0, tm, issue, 0, unroll=8)
    n = tm * TOK_TILES
    for k in range(TOP_K):
        pltpu.make_async_copy(ys_ref.at[pl.ds(0, n), :], buf_ref.at[k], sem).wait()
    mf = mf_ref[...]
    g1, g2 = mf[:, 0:1], mf[:, 1:2]
    for s in range(TOK_TILES):
        y_ref[:, s * LANES:(s + 1) * LANES] = (
            buf_ref[0, pl.ds(s, tm, stride=TOK_TILES), :] * g1 +
            buf_ref[1, pl.ds(s, tm, stride=TOK_TILES), :] * g2)
    o_ref[...] = _rmsnorm(x_ref[...] + y_ref[...], g_ref[...])


def moe_combine(ys, dest, meta_f, x, g_final, tm_pref=512):
    T, D = x.shape
    tm = _tile(T, tm_pref)
    return pl.pallas_call(
        functools.partial(_combine_kernel, tm), grid=(T // tm,),
        in_specs=[pl.BlockSpec((TOP_K, tm), lambda i: (0, i), memory_space=pltpu.SMEM),
                  pl.BlockSpec(memory_space=pl.ANY),
                  pl.BlockSpec((tm, LANES), lambda i: (i, 0)),
                  pl.BlockSpec((tm, D), lambda i: (i, 0)),
                  pl.BlockSpec((1, D), lambda i: (0, 0))],
        out_specs=pl.BlockSpec((tm, D), lambda i: (i, 0)),
        out_shape=jax.ShapeDtypeStruct((T, D), F32),
        scratch_shapes=[pltpu.VMEM((TOP_K, tm * TOK_TILES, LANES), F32), pltpu.VMEM((tm, D), F32),
                        pltpu.SemaphoreType.DMA(())],
        compiler_params=_params("arbitrary"), name="moe_combine",
    )(dest, ys, meta_f, x, g_final.reshape(1, D))


def moe_layer(x, g_ffn, g_final, w_router, w_gate, w_up, w_down):
    T = x.shape[0]
    tm = MOE_ROWS_LARGE if T * TOP_K >= MOE_ROWS_LARGE * N_EXPERTS * 8 else MOE_ROWS_SMALL
    hs, meta_i, meta_f, cnt = moe_router(x, g_ffn, w_router)
    counts = cnt[0, :N_EXPERTS].astype(I32)
    padded = (counts + tm - 1) // tm * tm
    pad_end = jnp.cumsum(padded)
    pad_start = pad_end - padded
    idx = meta_i[:, 0:TOP_K].T
    rank = meta_i[:, TOP_K:2 * TOP_K].T
    first_row = jnp.sum(jnp.where(idx[..., None] == jnp.arange(N_EXPERTS, dtype=I32), pad_start, 0), axis=-1)
    dest = first_row + rank
    nb = (T * TOP_K) // tm + N_EXPERTS
    n_used = (pad_end[-1] // tm).astype(I32)
    blk_row = jnp.minimum(jnp.arange(nb, dtype=I32), n_used - 1) * tm
    blk_expert = jnp.minimum(jnp.searchsorted(pad_end, blk_row, side='right'), N_EXPERTS - 1).astype(I32)
    xs = moe_dispatch(hs, dest, nb * tm)
    ys = moe_experts(xs, blk_expert, n_used.reshape(1), w_gate, w_up, w_down, tm)
    return moe_combine(ys, dest, meta_f, x, g_final)


def _prep_weights(g_mix, g_mem, g_ffn, g_final, a_w_in, a_conv, a_log, a_dt_bias, a_o_norm,
                  a_mem_kv, a_w_out, b_w_in, b_sink, b_mem_kv, b_w_out, f_w_gate, f_w_up, f_w_down,
                  e_router, e_w_gate, e_w_up, e_w_down):
    s1 = DN_QKV
    s2 = s1 + DN_V
    s3 = s2 + 4 * DN_HEADS
    w = a_w_in[0]
    gates_w = jnp.zeros((D_MODEL, LANES), F32).at[:, :4 * DN_HEADS].set(w[:, s2:s3])
    wb = b_w_in[0]
    return dict(
        g_mix=g_mix, g_mem=g_mem, g_ffn=g_ffn, g_final=g_final,
        a_in=[w[:, :s1].astype(BF16), w[:, s1:s2].astype(BF16), w[:, s3:].astype(BF16), gates_w],
        a_conv=a_conv[0], a_log=a_log[0], a_dt_bias=a_dt_bias[0], a_o_norm=a_o_norm[0],
        a_mem_kv=a_mem_kv[0].astype(BF16), a_w_out=a_w_out[0].astype(BF16),
        b_in=[wb[:, :WA_Q].astype(BF16), wb[:, WA_Q:WA_Q + WA_KV].astype(BF16),
              wb[:, WA_Q + WA_KV:WA_Q + 2 * WA_KV].astype(BF16), wb[:, WA_Q + 2 * WA_KV:].astype(BF16)],
        b_sink=b_sink[0], b_mem_kv=b_mem_kv[0].astype(BF16), b_w_out=b_w_out[0].astype(BF16),
        f_w_gate=f_w_gate[0].astype(BF16), f_w_up=f_w_up[0].astype(BF16), f_w_down=f_w_down[0].astype(BF16),
        e_router=e_router[0], e_w_gate=e_w_gate[0].astype(BF16), e_w_up=e_w_up[0].astype(BF16),
        e_w_down=e_w_down[0].astype(BF16))


def _trunk(x, mem, p):
    B, L, D = x.shape
    M = mem.shape[1]
    T = B * L
    x2 = x.reshape(T, D)
    mem2 = mem.reshape(B * M, D)

    qkv, z, qc, gates = norm_proj(x2, p['g_mix'][0], p['a_in'], [BF16, BF16, BF16, F32])
    (kv,) = norm_proj(mem2, p['g_mem'][0], [p['a_mem_kv']], [BF16])
    q, k, v, pack, packt = dn_prep(qkv.reshape(B, L, DN_QKV), gates.reshape(B, L, LANES),
                                   p['a_conv'], p['a_log'], p['a_dt_bias'])
    o_f, o_b = delta_rule(q, k, v, pack, packt)
    xo = mem_attention(qc.reshape(B, L, MX_W), kv.reshape(B, M, 2 * MX_W))
    x1 = out_proj_a(o_f, o_b, z.reshape(B, L, DN_V), xo, x, p['a_o_norm'], p['a_w_out'])
    x1 = ffn(x1.reshape(T, D), p['g_ffn'][0], p['f_w_gate'], p['f_w_up'], p['f_w_down'])

    cos, sin_signed = _rope_tables(L)
    q, k, v, qc = norm_proj(x1, p['g_mix'][1], p['b_in'], [BF16, BF16, BF16, BF16],
                            rope=(cos, sin_signed, (WA_HEAD_DIM ** -0.5, 1.0, None, None)))
    (kv,) = norm_proj(mem2, p['g_mem'][1], [p['b_mem_kv']], [BF16])
    o = window_attention(q.reshape(B, L, WA_Q), k.reshape(B, L, WA_KV), v.reshape(B, L, WA_KV),
                         p['b_sink'])
    xo = mem_attention(qc.reshape(B, L, MX_W), kv.reshape(B, M, 2 * MX_W))
    x2 = out_proj_b(o, xo, x1.reshape(B, L, D), p['b_w_out'])
    y = moe_layer(x2.reshape(T, D), p['g_ffn'][1], p['g_final'], p['e_router'],
                  p['e_w_gate'], p['e_w_up'], p['e_w_down'])
    return y.reshape(B, L, D)


def kernel(x_prompt, x_sample, mem_prompt, mem_sample, g_mix, g_mem, g_ffn, g_final, a_w_in, a_conv,
           a_log, a_dt_bias, a_o_norm, a_mem_kv, a_w_out, b_w_in, b_sink, b_mem_kv, b_w_out,
           f_w_gate, f_w_up, f_w_down, e_router, e_w_gate, e_w_up, e_w_down):
    p = _prep_weights(g_mix, g_mem, g_ffn, g_final, a_w_in, a_conv, a_log, a_dt_bias, a_o_norm,
                      a_mem_kv, a_w_out, b_w_in, b_sink, b_mem_kv, b_w_out, f_w_gate, f_w_up, f_w_down,
                      e_router, e_w_gate, e_w_up, e_w_down)
    return (_trunk(x_prompt, mem_prompt, p), _trunk(x_sample, mem_sample, p))
```

```python
import functools
import math

import jax
import jax.numpy as jnp
from jax import lax
from jax.experimental import pallas as pl
from jax.experimental.pallas import tpu as pltpu

F32 = jnp.float32
BF16 = jnp.bfloat16
I32 = jnp.int32

EPS = 1e-6
LOG2_E = math.log2(math.e)
D_MODEL = 1024
LANES = 128
SUBLANES = 8
TOK_TILES = D_MODEL // LANES
VMEM_LIMIT_BYTES = 56 * 1024 * 1024

DN_HEADS = 6
DN_D = 128
DN_CONV = 5
DN_CHUNK = 64
DN_QKV = 3 * DN_HEADS * DN_D
DN_V = DN_HEADS * DN_D
DN_TILE = 256
CONV_HALO = 16

WA_Q_HEADS = 12
WA_KV_HEADS = 4
WA_GROUP = WA_Q_HEADS // WA_KV_HEADS
WA_HEAD_DIM = 64
WA_BLOCK = 128
WA_QBLOCK = 256
WA_Q = WA_Q_HEADS * WA_HEAD_DIM
WA_KV = WA_KV_HEADS * WA_HEAD_DIM
ROPE_THETA = 10000.0

MX_HEADS = 4
MX_HEAD_DIM = 64
MX_W = MX_HEADS * MX_HEAD_DIM

D_FF = 3584
N_EXPERTS = 8
TOP_K = 2
FF_TILE = 512
MOE_ROWS_SMALL = 512
MOE_ROWS_LARGE = 1024
ZERO_ROWS = 64


def _params(*sem):
    return pltpu.CompilerParams(dimension_semantics=sem, vmem_limit_bytes=VMEM_LIMIT_BYTES)


def _tile(n, pref):
    t = min(n, pref)
    assert n % t == 0, (n, pref)
    return t


def _dot(a, b):
    return jnp.dot(a, b, preferred_element_type=F32)


def _dot_nt(a, b):
    return lax.dot_general(a, b, (((1,), (1,)), ((), ())), preferred_element_type=F32)


def _dot_tn(a, b):
    return lax.dot_general(a, b, (((0,), (0,)), ((), ())), preferred_element_type=F32)


def _split3(x):
    hi = x.astype(BF16)
    r1 = x - hi.astype(F32)
    mid = r1.astype(BF16)
    lo = (r1 - mid.astype(F32)).astype(BF16)
    return hi, mid, lo


def _dot_x3(a, b):
    a_hi = a.astype(BF16)
    a_lo = (a - a_hi.astype(F32)).astype(BF16)
    b_hi = b.astype(BF16)
    b_lo = (b - b_hi.astype(F32)).astype(BF16)
    return _dot(a_hi, b_hi) + _dot(a_lo, b_hi) + _dot(a_hi, b_lo)


def _dot_mask_f32(mask_bf16, x):
    hi, mid, lo = _split3(x)
    return _dot(mask_bf16, hi) + _dot(mask_bf16, mid) + _dot(mask_bf16, lo)


def _rmsnorm(x, g):
    ms = jnp.mean(x * x, axis=-1, keepdims=True)
    return x * lax.rsqrt(ms + EPS) * g


def _silu(x):
    return x * jax.nn.sigmoid(x)


def _rope(x, cos, sin_signed):
    n, w = x.shape
    lane = lax.broadcasted_iota(I32, (n, LANES), 1)
    first_half = (lane % WA_HEAD_DIM) < (WA_HEAD_DIM // 2)
    outs = []
    for gidx in range(w // LANES):
        xs = x[:, gidx * LANES:(gidx + 1) * LANES]
        rot = jnp.where(first_half, pltpu.roll(xs, LANES - WA_HEAD_DIM // 2, 1),
                        pltpu.roll(xs, WA_HEAD_DIM // 2, 1))
        outs.append(xs * cos + rot * sin_signed)
    return jnp.concatenate(outs, axis=1)


def _norm_proj_kernel(n_out, scales, rope_flags, x_ref, g_ref, *refs):
    if rope_flags is not None:
        cos_ref, sin_ref, refs = refs[0], refs[1], refs[2:]
    w_refs, o_refs = refs[:n_out], refs[n_out:]
    hf = _rmsnorm(x_ref[...], g_ref[...])
    hb = hf.astype(BF16)
    for idx, (w_ref, o_ref) in enumerate(zip(w_refs, o_refs)):
        if w_ref.dtype == F32:
            y = _dot_x3(hf, w_ref[...])
        else:
            y = _dot(hb, w_ref[...])
        if rope_flags is not None and rope_flags[idx]:
            y = _rope(y, cos_ref[...], sin_ref[...])
        if scales is not None and scales[idx] is not None:
            y = y * scales[idx]
        o_ref[...] = y.astype(o_ref.dtype)


def norm_proj(x, g, weights, out_dtypes, scales=None, rope=None, tm_pref=512):
    T, D = x.shape
    tm = _tile(T, tm_pref)
    n_out = len(weights)
    args = [x, g.reshape(1, D)]
    extra_specs = []
    flags = None
    if rope is not None:
        cos, sin_signed, flags = rope
        tm = _tile(cos.shape[0], tm)
        per_seq = cos.shape[0] // tm
        extra_specs = [pl.BlockSpec((tm, LANES), lambda i: (i % per_seq, 0))] * 2
        args += [cos, sin_signed]
    in_specs = [pl.BlockSpec((tm, D), lambda i: (i, 0)), pl.BlockSpec((1, D), lambda i: (0, 0))]
    in_specs += extra_specs + [pl.BlockSpec(w.shape, lambda i: (0, 0)) for w in weights]
    out_specs = [pl.BlockSpec((tm, w.shape[1]), lambda i: (i, 0)) for w in weights]
    out_shape = [jax.ShapeDtypeStruct((T, w.shape[1]), dt) for w, dt in zip(weights, out_dtypes)]
    return pl.pallas_call(
        functools.partial(_norm_proj_kernel, n_out, scales, flags),
        grid=(T // tm,), in_specs=in_specs, out_specs=out_specs, out_shape=out_shape,
        compiler_params=_params("parallel"), name="norm_proj",
    )(*args, *weights)


def _dn_prep_kernel(tl, cur_ref, prev_ref, next_ref, gates_ref, cw_ref, alog_ref, dtb_ref, shift_ref,
                    q_ref, k_ref, v_ref, pack_ref, packt_ref):
    i = pl.program_id(1)
    nt = pl.num_programs(1)
    pad = (DN_CONV - 1) // 2
    taps = [j for j in range(DN_CONV) if j != pad]
    p8 =prev_ref[CONV_HALO - SUBLANES:, :].astype(F32) * (i > 0).astype(F32)
    n8 = next_ref[0:SUBLANES, :].astype(F32) * (i < nt - 1).astype(F32)
    row8 = lax.broadcasted_iota(I32, (SUBLANES, 1), 0)
    first = None
    last = None
    for j in range(DN_CONV):
        off = j - pad
        if off < 0:
            t = jnp.where(row8 < -off, pltpu.roll(p8, -off, 0), 0.0) * cw_ref[j:j + 1, :]
            first = t if first is None else first + t
        elif off > 0:
            t = jnp.where(row8 >= SUBLANES - off, pltpu.roll(n8, SUBLANES - off, 0), 0.0) * cw_ref[j:j + 1, :]
            last = t if last is None else last + t
    mxu_cols = 2 * LANES
    moved = None
    for cg in range(DN_QKV // LANES):
        sl = slice(cg * LANES, (cg + 1) * LANES)
        if cg % 2 == 0:
            x16 = cur_ref[:, cg * LANES:cg * LANES + mxu_cols]
            moved = {j: _dot(shift_ref[n], x16) for n, j in enumerate(taps)}
            moved[pad] = x16.astype(F32)
        half = slice((cg % 2) * LANES, (cg % 2 + 1) * LANES)
        acc = None
        for j in range(DN_CONV):
            term = moved[j][:, half] * cw_ref[j:j + 1, sl]
            acc = term if acc is None else acc + term
        acc = acc + jnp.concatenate(
            [first[:, sl], jnp.zeros((tl - 2 * SUBLANES, LANES), F32), last[:, sl]], axis=0)
        y = _silu(acc)
        if cg < DN_HEADS:
            y = y * lax.rsqrt(jnp.sum(y * y, axis=-1, keepdims=True) + EPS) * (DN_D ** -0.5)
            q_ref[cg] = y.astype(q_ref.dtype)
        elif cg < 2 * DN_HEADS:
            y = y * lax.rsqrt(jnp.sum(y * y, axis=-1, keepdims=True) + EPS)
            k_ref[cg - DN_HEADS] = y.astype(k_ref.dtype)
        else:
            v_ref[cg - 2 * DN_HEADS] = y.astype(v_ref.dtype)

    gt = gates_ref[...]
    lane = lax.broadcasted_iota(I32, gt.shape, 1)
    beta = jax.nn.sigmoid(gt)
    xg = gt + dtb_ref[...]
    softplus = jnp.maximum(xg, 0.0) + jnp.log1p(jnp.exp(-jnp.abs(xg)))
    g = -jnp.exp(alog_ref[...]) * softplus
    g = jnp.where((lane >= 2 * DN_HEADS) & (lane < 4 * DN_HEADS), g, 0.0)
    r = lax.broadcasted_iota(I32, (tl, tl), 0)
    c = lax.broadcasted_iota(I32, (tl, tl), 1)
    same = (r // DN_CHUNK) == (c // DN_CHUNK)
    m_fwd = (same & (r >= c)).astype(BF16)
    m_bwd = (same & (r <= c)).astype(BF16)
    m_all = same.astype(BF16)
    hi, mid, lo = _split3(g)
    g_fwd = _dot(m_fwd, hi) + _dot(m_fwd, mid) + _dot(m_fwd, lo)
    g_bwd = _dot(m_bwd, hi) + _dot(m_bwd, mid) + _dot(m_bwd, lo)
    g_tot = pltpu.roll(_dot(m_all, hi) + _dot(m_all, mid) + _dot(m_all, lo), 2 * DN_HEADS, 1)
    pack = jnp.where(lane < 2 * DN_HEADS, beta,
                     jnp.where(lane < 3 * DN_HEADS, g_fwd,
                               jnp.where(lane < 4 * DN_HEADS, g_bwd,
                                         jnp.where(lane < 6 * DN_HEADS, g_tot, 0.0))))
    pack_ref[...] = pack
    packt_ref[...] = pack.T


def dn_prep(qkv, gates, conv_w, a_log, dt_bias):
    B, L, _ = qkv.shape
    tl = _tile(L, DN_TILE)
    nt = L // tl
    hb = tl // CONV_HALO
    nhalo = L // CONV_HALO
    cw = jnp.zeros((SUBLANES, DN_QKV), F32).at[:DN_CONV].set(conv_w)
    alog = jnp.zeros((1, LANES), F32).at[0, 2 * DN_HEADS:4 * DN_HEADS].set(a_log.reshape(-1))
    dtb = jnp.zeros((1, LANES), F32).at[0, 2 * DN_HEADS:4 * DN_HEADS].set(dt_bias.reshape(-1))
    pad = (DN_CONV - 1) // 2
    t_idx = jnp.arange(tl)
    shift = jnp.stack([(t_idx[None, :] == t_idx[:, None] + (j - pad)).astype(BF16)
                       for j in range(DN_CONV) if j != pad])
    hm = jax.ShapeDtypeStruct((B, DN_HEADS, L, DN_D), BF16)
    hm_spec = pl.BlockSpec((None, DN_HEADS, tl, DN_D), lambda b, i: (b, 0, i, 0))
    return pl.pallas_call(
        functools.partial(_dn_prep_kernel, tl),
        grid=(B, nt),
        in_specs=[
            pl.BlockSpec((None, tl, DN_QKV), lambda b, i: (b, i, 0)),
            pl.BlockSpec((None, CONV_HALO, DN_QKV), lambda b, i: (b, jnp.maximum(i * hb - 1, 0), 0)),
            pl.BlockSpec((None, CONV_HALO, DN_QKV), lambda b, i: (b, jnp.minimum((i + 1) * hb, nhalo - 1), 0)),
            pl.BlockSpec((None, tl, LANES), lambda b, i: (b, i, 0)),
            pl.BlockSpec((SUBLANES, DN_QKV), lambda b, i: (0, 0)),
            pl.BlockSpec((1, LANES), lambda b, i: (0, 0)),
            pl.BlockSpec((1, LANES), lambda b, i: (0, 0)),
            pl.BlockSpec(shift.shape, lambda b, i: (0, 0, 0)),
        ],
        out_specs=[hm_spec, hm_spec, hm_spec,
                   pl.BlockSpec((None, tl, LANES), lambda b, i: (b, i, 0)),
                   pl.BlockSpec((None, LANES, tl), lambda b, i: (b, 0, i))],
        out_shape=[hm, hm, hm, jax.ShapeDtypeStruct((B, L, LANES), F32),
                   jax.ShapeDtypeStruct((B, LANES, L), F32)],
        compiler_params=_params("parallel", "parallel"), name="dn_prep",
    )(qkv, qkv, qkv, gates, cw, alog, dtb, shift)


def _dn_tile_stage(units, tl, refs_f, refs_b, masks_f, masks_b, u_ref, w_ref, qd_ref, kd_ref, aqk_ref):
    n = len(units)
    nchunk = tl // DN_CHUNK
    refs = [refs_f if d == 0 else refs_b for d, _ in units]
    masks = [masks_f if d == 0 else masks_b for d, _ in units]
    col = []
    for (d, h), (q_ref, k_ref, v_ref, pack_ref, packt_ref) in zip(units, refs):
        pick = lambda li: pack_ref[:, li:li + 1]
        li_g = (2 + d) * DN_HEADS + h
        col.append((pick(d * DN_HEADS + h), pick(li_g), pick((4 + d) * DN_HEADS + h),
                    packt_ref[li_g:li_g + 1, :]))
    q16 = [r[0][h] for (d, h), r in zip(units, refs)]
    k16 = [r[1][h] for (d, h), r in zip(units, refs)]
    kf = [x.astype(F32) for x in k16]
    kb = [kf[i] * col[i][0] for i in range(n)]
    decay = [jnp.exp(jnp.where(masks[i][0], col[i][1] - col[i][3], -jnp.inf)) for i in range(n)]
    kk = [_dot_nt(kb[i].astype(BF16), k16[i]) for i in range(n)]
    qk = [_dot_nt(q16[i], k16[i]) for i in range(n)]
    a = [jnp.where(masks[i][1], kk[i] * decay[i], 0.0) for i in range(n)]
    for i, (d, h) in enumerate(units):
        a_qk = (qk[i] * decay[i]).astype(BF16)
        for c in range(nchunk):
            rs = slice(c * DN_CHUNK, (c + 1) * DN_CHUNK)
            aqk_ref[d * DN_HEADS + h, rs, :] = a_qk[rs, rs]
    same, eye_wide = masks[0][2], masks[0][3]

    def wide(x):
        out = x[0:DN_CHUNK]
        for c in range(1, nchunk):
            out = out + x[c * DN_CHUNK:(c + 1) * DN_CHUNK]
        return out

    def block_diag(x16):
        return jnp.where(same, jnp.concatenate([x16] * nchunk, axis=0), jnp.zeros((), BF16))

    t_w = [eye_wide - wide(x) for x in a]
    p_w16 = [wide(x).astype(BF16) for x in a]
    p_bd16 = [x.astype(BF16) for x in a]
    for it in range(int(math.log2(DN_CHUNK)) - 1):
        p_w16 = [_dot(p_w16[i], p_bd16[i]).astype(BF16) for i in range(n)]
        p_bd16 = [block_diag(x) for x in p_w16]
        t_w = [t_w[i] + _dot(t_w[i].astype(BF16), p_bd16[i]) for i in range(n)]
    t_bd16 = [block_diag(x.astype(BF16)) for x in t_w]
    e_g = [jnp.exp(col[i][1]) for i in range(n)]
    rhs = [jnp.concatenate([refs[i][2][units[i][1]].astype(F32) * col[i][0], kb[i] * e_g[i]],
                           axis=1).astype(BF16) for i in range(n)]
    uw = [_dot(t_bd16[i], rhs[i]) for i in range(n)]
    for i, (d, h) in enumerate(units):
        ui = d * DN_HEADS + h
        u_ref[ui] = uw[i][:, :DN_D]
        w_ref[ui] = uw[i][:, DN_D:].astype(BF16)
        qd_ref[ui] = (q16[i].astype(F32) * e_g[i]).astype(BF16)
        kd_ref[ui] = (kf[i] * jnp.exp(col[i][2] - col[i][1])).astype(BF16)


def _dn_scan_stage(tl, ptf_ref, ptb_ref, of_ref, ob_ref, s_ref, u_ref, w_ref, qd_ref, kd_ref, aqk_ref):
    nchunk = tl // DN_CHUNK
    chains = [(d, h) for h in range(DN_HEADS) for d in range(2)]
    ids = [d * DN_HEADS + h for d, h in chains]
    for step in range(nchunk):
        rows = []
        for d, h in chains:
            c = step if d == 0 else nchunk - 1 - step
            rows.append(slice(c * DN_CHUNK, (c + 1) * DN_CHUNK))
        s = [s_ref[ui] for ui in ids]
        sb = [x.astype(BF16) for x in s]
        v_new = [(u_ref[ui, rs, :] - _dot(w_ref[ui, rs, :], b)).astype(BF16)
                 for ui, rs, b in zip(ids, rows, sb)]
        o = [_dot(qd_ref[ui, rs, :], b) + _dot(aqk_ref[ui, rs, :], vn)
             for ui, rs, b, vn in zip(ids, rows, sb, v_new)]
        for (d, h), ui, rs, x, vn, oo in zip(chains, ids, rows, s, v_new, o):
            pt_ref = ptf_ref if d == 0 else ptb_ref
            li_tot = (4 + d) * DN_HEADS + h
            tot = pt_ref[li_tot:li_tot + 1, rs.start:rs.start + 1]
            s_ref[ui] = x * jnp.exp(tot) + _dot_tn(kd_ref[ui, rs, :], vn)
            (of_ref if d == 0 else ob_ref)[h, rs, :] = oo.astype(of_ref.dtype)


def _dn_kernel(tl, qf_ref, kf_ref, vf_ref, pf_ref, ptf_ref, qb_ref, kb_ref, vb_ref, pb_ref, ptb_ref,
               of_ref, ob_ref, s_ref, u_ref, w_ref, qd_ref, kd_ref, aqk_ref):
    @pl.when(pl.program_id(1) == 0)
    def _():
        s_ref[...] = jnp.zeros_like(s_ref)

    r = lax.broadcasted_iota(I32, (tl, tl), 0)
    c = lax.broadcasted_iota(I32, (tl, tl), 1)
    same = (r // DN_CHUNK) == (c // DN_CHUNK)
    eye_wide = (lax.broadcasted_iota(I32, (DN_CHUNK, tl), 0) ==
                lax.broadcasted_iota(I32, (DN_CHUNK, tl), 1) % DN_CHUNK).astype(F32)
    masks_f = (same & (r >= c), same & (r > c), same, eye_wide)
    masks_b = (same & (r <= c), same & (r < c), same, eye_wide)
    refs_f = (qf_ref, kf_ref, vf_ref, pf_ref, ptf_ref)
    refs_b = (qb_ref, kb_ref, vb_ref, pb_ref, ptb_ref)

    units = [(d, h) for h in range(DN_HEADS) for d in range(2)]
    _dn_tile_stage(units, tl, refs_f, refs_b, masks_f, masks_b, u_ref, w_ref, qd_ref, kd_ref, aqk_ref)
    _dn_scan_stage(tl, ptf_ref, ptb_ref, of_ref, ob_ref, s_ref, u_ref, w_ref, qd_ref, kd_ref, aqk_ref)


def delta_rule(q, k, v, pack, packt):
    B, H, L, _ = q.shape
    tl = _tile(L, DN_TILE)
    nt = L // tl
    fwd = lambda b, i: (b, 0, i, 0)
    bwd = lambda b, i: (b, 0, nt - 1 - i, 0)
    hm_f = pl.BlockSpec((None, H, tl, DN_D), fwd)
    hm_b = pl.BlockSpec((None, H, tl, DN_D), bwd)
    pk_f = pl.BlockSpec((None, tl, LANES), lambda b, i: (b, i, 0))
    pk_b = pl.BlockSpec((None, tl, LANES), lambda b, i: (b, nt - 1 - i, 0))
    pt_f = pl.BlockSpec((None, LANES, tl), lambda b, i: (b, 0, i))
    pt_b = pl.BlockSpec((None, LANES, tl), lambda b, i: (b, 0, nt - 1 - i))
    o = jax.ShapeDtypeStruct((B, H, L, DN_D), BF16)
    return pl.pallas_call(
        functools.partial(_dn_kernel, tl),
        grid=(B, nt),
        in_specs=[hm_f, hm_f, hm_f, pk_f, pt_f, hm_b, hm_b, hm_b, pk_b, pt_b],
        out_specs=[hm_f, hm_b], out_shape=[o, o],
        scratch_shapes=[pltpu.VMEM((2 * H, DN_D, DN_D), F32),
                        pltpu.VMEM((2 * H, tl, DN_D), F32),
                        pltpu.VMEM((2 * H, tl, DN_D), BF16),
                        pltpu.VMEM((2 * H, tl, DN_D), BF16),
                        pltpu.VMEM((2 * H, tl, DN_D), BF16),
                        pltpu.VMEM((2 * H, tl, DN_CHUNK), BF16)],
        compiler_params=_params("parallel", "arbitrary"), name="delta_rule",
    )(q, k, v, pack, packt, q, k, v, pack, packt)


def _mem_attn_kernel(q_ref, kv_ref, o_ref):
    q = q_ref[...]
    kv = kv_ref[...]
    outs = []
    for h in range(MX_HEADS):
        hs = slice(h * MX_HEAD_DIM, (h + 1) * MX_HEAD_DIM)
        vs = slice(MX_W + h * MX_HEAD_DIM, MX_W + (h + 1) * MX_HEAD_DIM)
        s = _dot_nt(kv[:, hs], q[:, hs])
        m = jnp.max(s, axis=0, keepdims=True)
        p = jnp.exp2(s - m)
        den = jnp.sum(p, axis=0, keepdims=True)
        outs.append((_dot_tn(kv[:, vs], p.astype(BF16)) / den).T)
    o_ref[...] = jnp.concatenate(outs, axis=1).astype(o_ref.dtype)


def mem_attention(qc, kv, tl_pref=512):
    B, L, _ = qc.shape
    M = kv.shape[1]
    tl = _tile(L, tl_pref)
    return pl.pallas_call(
        _mem_attn_kernel, grid=(B, L // tl),
        in_specs=[pl.BlockSpec((None, tl, MX_W), lambda b, i: (b, i, 0)),
                  pl.BlockSpec((None, M, 2 * MX_W), lambda b, i: (b, 0, 0))],
        out_specs=pl.BlockSpec((None, tl, MX_W), lambda b, i: (b, i, 0)),
        out_shape=jax.ShapeDtypeStruct((B, L, MX_W), BF16),
        compiler_params=_params("parallel", "parallel"), name="mem_attention",
    )(qc, kv)


def _out_proj_a_kernel(of_ref, ob_ref, z_ref, xo_ref, x_ref, on_ref, w_ref, o_ref):
    parts = []
    z = z_ref[...].astype(F32)
    for h in range(DN_HEADS):
        o = of_ref[h].astype(F32) + ob_ref[h].astype(F32)
        y = _rmsnorm(o, on_ref[...]) * _silu(z[:, h * DN_D:(h + 1) * DN_D])
        parts.append(y.astype(BF16))
    parts.append(xo_ref[...])
    mixed = jnp.concatenate(parts, axis=1)
    o_ref[...] = x_ref[...] + _dot(mixed, w_ref[...])


def out_proj_a(o_f, o_b, z, xo, x, o_norm, w_out, tm_pref=512):
    B, H, L, _ = o_f.shape
    D = x.shape[-1]
    tm = _tile(L, tm_pref)
    hm = pl.BlockSpec((None, H, tm, DN_D), lambda b, i: (b, 0, i, 0))
    row = lambda w: pl.BlockSpec((None, tm, w), lambda b, i: (b, i, 0))
    return pl.pallas_call(
        _out_proj_a_kernel, grid=(B, L // tm),
        in_specs=[hm, hm, row(DN_V), row(MX_W), row(D),
                  pl.BlockSpec((1, DN_D), lambda b, i: (0, 0)),
                  pl.BlockSpec(w_out.shape, lambda b, i: (0, 0))],
        out_specs=row(D), out_shape=jax.ShapeDtypeStruct((B, L, D), F32),
        compiler_params=_params("parallel", "parallel"), name="out_proj_a",
    )(o_f, o_b, z, xo, x, o_norm.reshape(1, DN_D), w_out)


def _out_proj_b_kernel(o_ref, xo_ref, x_ref, w_ref, out_ref):
    mixed = jnp.concatenate([o_ref[...], xo_ref[...]], axis=1)
    out_ref[...] = x_ref[...] + _dot(mixed, w_ref[...])


def out_proj_b(o, xo, x, w_out, tm_pref=512):
    B, L, D = x.shape
    tm = _tile(L, tm_pref)
    row = lambda w: pl.BlockSpec((None, tm, w), lambda b, i: (b, i, 0))
    return pl.pallas_call(
        _out_proj_b_kernel, grid=(B, L // tm),
        in_specs=[row(WA_Q), row(MX_W), row(D), pl.BlockSpec(w_out.shape, lambda b, i: (0, 0))],
        out_specs=row(D), out_shape=jax.ShapeDtypeStruct((B, L, D), F32),
        compiler_params=_params("parallel", "parallel"), name="out_proj_b",
    )(o, xo, x, w_out)


def _swiglu_step(h, wg_ref, wu_ref, wd_ref, acc_ref):
    gate = _dot(h, wg_ref[...])
    up = _dot(h, wu_ref[...])
    acc_ref[...] += _dot((_silu(gate) * up).astype(BF16), wd_ref[...])


def _ffn_kernel(x_ref, g_ref, wg_ref, wu_ref, wd_ref, o_ref, h_ref, acc_ref):
    j = pl.program_id(1)

    @pl.when(j == 0)
    def _():
        h_ref[...] = _rmsnorm(x_ref[...], g_ref[...]).astype(BF16)
        acc_ref[...] = jnp.zeros_like(acc_ref)

    _swiglu_step(h_ref[...], wg_ref, wu_ref, wd_ref, acc_ref)

    @pl.when(j == pl.num_programs(1) - 1)
    def _():
        o_ref[...] = x_ref[...] + acc_ref[...]


def ffn(x, g, w_gate, w_up, w_down, tm_pref=1024):
    T, D = x.shape
    F = w_gate.shape[1]
    tm = _tile(T, tm_pref)
    tf = _tile(F, FF_TILE)
    return pl.pallas_call(
        _ffn_kernel, grid=(T // tm, F // tf),
        in_specs=[pl.BlockSpec((tm, D), lambda i, j: (i, 0)),
                  pl.BlockSpec((1, D), lambda i, j: (0, 0)),
                  pl.BlockSpec((D, tf), lambda i, j: (0, j)),
                  pl.BlockSpec((D, tf), lambda i, j: (0, j)),
                  pl.BlockSpec((tf, D), lambda i, j: (j, 0))],
        out_specs=pl.BlockSpec((tm, D), lambda i, j: (i, 0)),
        out_shape=jax.ShapeDtypeStruct((T, D), F32),
        scratch_shapes=[pltpu.VMEM((tm, D), BF16), pltpu.VMEM((tm, D), F32)],
        compiler_params=_params("parallel", "arbitrary"), name="ffn",
    )(x, g.reshape(1, D), w_gate, w_up, w_down)


def _win_attn_kernel(sink_ref, q_ref, kp_ref, kc_ref, kn_ref, vp_ref, vc_ref, vn_ref, bias_ref, o_ref):
    i = pl.program_id(1)
    n = pl.num_programs(1)
    qb = q_ref.shape[0]
    q = q_ref[...]
    k = jnp.concatenate([kp_ref[...], kc_ref[...], kn_ref[...]], axis=0)
    v = jnp.concatenate([vp_ref[...], vc_ref[...], vn_ref[...]], axis=0)
    bias = bias_ref[(i == 0).astype(I32) + 2 * (i == n - 1).astype(I32)]
    bias = jnp.concatenate([bias] * WA_GROUP, axis=1)
    cols = WA_GROUP * qb
    grp = lax.broadcasted_iota(I32, (1, cols), 1) // qb
    outs = [None] * WA_Q_HEADS
    for h in range(WA_KV_HEADS):
        hs = slice(h * WA_HEAD_DIM, (h + 1) * WA_HEAD_DIM)
        qs = jnp.concatenate(
            [q[:, (h * WA_GROUP + g) * WA_HEAD_DIM:(h * WA_GROUP + g + 1) * WA_HEAD_DIM]
             for g in range(WA_GROUP)], axis=0)
        sk = jnp.zeros((1, cols), F32)
        for g in range(WA_GROUP):
            sk = jnp.where(grp == g, sink_ref[h * WA_GROUP + g] * LOG2_E, sk)
        s = _dot_nt(k[:, hs], qs) + bias
        m = jnp.maximum(jnp.max(s, axis=0, keepdims=True), sk)
        p = jnp.exp2(s - m)
        den = jnp.sum(p, axis=0, keepdims=True) + jnp.exp2(sk - m)
        o_t = _dot_tn(v[:, hs], p.astype(BF16)) / den
        o = o_t.T
        for g in range(WA_GROUP):
            outs[h * WA_GROUP + g] = o[g * qb:(g + 1) * qb]
    o_ref[...] = jnp.concatenate(outs, axis=1).astype(o_ref.dtype)


def _window_bias(qb):
    nk = qb + 2 * WA_BLOCK
    kj = jnp.arange(nk)[:, None]
    qi = jnp.arange(qb)[None, :]
    band = jnp.abs(kj - WA_BLOCK - qi) <= WA_BLOCK
    not_prev = kj >= WA_BLOCK
    not_next = kj < WA_BLOCK + qb
    variants = [band, band & not_prev, band & not_next, band & not_prev & not_next]
    return jnp.stack([jnp.where(m, 0.0, -jnp.inf).astype(F32) for m in variants])


def window_attention(q, k, v, sink):
    B, L, _ = q.shape
    qb = _tile(L, WA_QBLOCK)
    n = L // qb
    per = qb // WA_BLOCK
    nhalo = L // WA_BLOCK
    prev = lambda b, i, s: (b, jnp.maximum(i * per - 1, 0), 0)
    cur = lambda b, i, s: (b, i, 0)
    nxt = lambda b, i, s: (b, jnp.minimum((i + 1) * per, nhalo - 1), 0)
    halo = lambda f: pl.BlockSpec((None, WA_BLOCK, WA_KV), f)
    body = pl.BlockSpec((None, qb, WA_KV), cur)
    bias = _window_bias(qb)
    grid_spec = pltpu.PrefetchScalarGridSpec(
        num_scalar_prefetch=1, grid=(B, n),
        in_specs=[pl.BlockSpec((None, qb, WA_Q), cur),
                  halo(prev), body, halo(nxt), halo(prev), body, halo(nxt),
                  pl.BlockSpec(bias.shape, lambda b, i, s: (0, 0, 0))],
        out_specs=pl.BlockSpec((None, qb, WA_Q), cur))
    return pl.pallas_call(
        _win_attn_kernel, grid_spec=grid_spec,
        out_shape=jax.ShapeDtypeStruct((B, L, WA_Q), BF16),
        compiler_params=_params("parallel", "parallel"), name="window_attention",
    )(sink, q, k, k, k, v, v, v, bias)


def _rope_tables(L):
    inv = ROPE_THETA ** (-jnp.arange(0, WA_HEAD_DIM, 2, dtype=F32) / WA_HEAD_DIM)
    ang = jnp.arange(L, dtype=F32)[:, None] * inv[None, :]
    cos, sin = jnp.cos(ang), jnp.sin(ang)
    reps = LANES // WA_HEAD_DIM
    return jnp.tile(jnp.concatenate([cos, cos], axis=1), (1, reps)), \
        jnp.tile(jnp.concatenate([-sin, sin], axis=1), (1, reps))


def _router_kernel(tm, x_ref, g_ref, wr_ref, hs_ref, mi_ref, mf_ref, cnt_ref, carry_ref):
    @pl.when(pl.program_id(0) == 0)
    def _():
        carry_ref[...] = jnp.zeros_like(carry_ref)

    hf = _rmsnorm(x_ref[...], g_ref[...])
    for s in range(TOK_TILES):
        hs_ref[pl.ds(s, tm, stride=TOK_TILES), :] = hf[:, s * LANES:(s + 1) * LANES]
    logits = _dot_x3(hf, wr_ref[...])
    lane = lax.broadcasted_iota(I32, logits.shape, 1)
    lg = jnp.where(lane < N_EXPERTS, logits, -jnp.inf)
    m1 = jnp.max(lg, axis=1, keepdims=True)
    i1 = jnp.min(jnp.where(lg == m1, lane, LANES), axis=1, keepdims=True)
    lg2 = jnp.where(lane == i1, -jnp.inf, lg)
    m2 = jnp.max(lg2, axis=1, keepdims=True)
    i2 = jnp.min(jnp.where(lg2 == m2, lane, LANES), axis=1, keepdims=True)
    e2 = jnp.exp(m2 - m1)
    g1 = 1.0 / (1.0 + e2)
    g2 = e2 / (1.0 + e2)
    oh1 = lane == i1
    oh2 = lane == i2
    oh = (oh1 | oh2).astype(F32)
    r = lax.broadcasted_iota(I32, (tm, tm), 0)
    c = lax.broadcasted_iota(I32, (tm, tm), 1)
    before = _dot((r > c).astype(BF16), oh.astype(BF16)) + carry_ref[...]
    rank1 = jnp.sum(jnp.where(oh1, before, 0.0), axis=1, keepdims=True).astype(I32)
    rank2 = jnp.sum(jnp.where(oh2, before, 0.0), axis=1, keepdims=True).astype(I32)
    carry_ref[...] += jnp.sum(oh, axis=0, keepdims=True)
    mi_ref[...] = jnp.where(lane == 0, i1, jnp.where(lane == 1, i2, jnp.where(lane == 2, rank1,
                            jnp.where(lane == 3, rank2, 0))))
    mf_ref[...] = jnp.where(lane == 0, g1, jnp.where(lane == 1, g2, 0.0))
    cnt_ref[...] = jnp.broadcast_to(carry_ref[...], cnt_ref.shape)


def moe_router(x, g, w_router, tm_pref=512):
    T, D = x.shape
    tm = _tile(T, tm_pref)
    wr = jnp.zeros((D, LANES), F32).at[:, :N_EXPERTS].set(w_router)
    return pl.pallas_call(
        functools.partial(_router_kernel, tm), grid=(T // tm,),
        in_specs=[pl.BlockSpec((tm, D), lambda i: (i, 0)),
                  pl.BlockSpec((1, D), lambda i: (0, 0)),
                  pl.BlockSpec((D, LANES), lambda i: (0, 0))],
        out_specs=[pl.BlockSpec((tm * TOK_TILES, LANES), lambda i: (i, 0)),
                   pl.BlockSpec((tm, LANES), lambda i: (i, 0)),
                   pl.BlockSpec((tm, LANES), lambda i: (i, 0)),
                   pl.BlockSpec((SUBLANES, LANES), lambda i: (0, 0))],
        out_shape=[jax.ShapeDtypeStruct((T * TOK_TILES, LANES), F32),
                   jax.ShapeDtypeStruct((T, LANES), I32),
                   jax.ShapeDtypeStruct((T, LANES), F32),
                   jax.ShapeDtypeStruct((SUBLANES, LANES), F32)],
        scratch_shapes=[pltpu.VMEM((1, LANES), F32)],
        compiler_params=_params("arbitrary"), name="moe_router",
    )(x, g.reshape(1, D), wr)


def _row_copy(src, src_row, dst, dst_row, sem):
    return pltpu.make_async_copy(src.at[pl.ds(src_row * TOK_TILES, TOK_TILES), :],
                                 dst.at[pl.ds(dst_row * TOK_TILES, TOK_TILES), :], sem)


def _dispatch_kernel(tm, tail_ref, dest_ref, hs_ref, xs_ref, zero_ref, sem, zero_sem):
    @pl.when(pl.program_id(0) == 0)
    def _():
        zero_ref[...] = jnp.zeros_like(zero_ref)
        for wait in (False, True):
            for e in range(N_EXPERTS):
                def pad_row(r, carry):
                    cp = _row_copy(zero_ref, 0, xs_ref, tail_ref[0, e] + r, zero_sem)
                    cp.wait() if wait else cp.start()
                    return carry

                lax.fori_loop(0, tail_ref[1, e], pad_row, 0)

            def pad_chunk(r, carry):
                row = tail_ref[0, N_EXPERTS] + r * ZERO_ROWS
                cp = pltpu.make_async_copy(
                    zero_ref, xs_ref.at[pl.ds(row * TOK_TILES, ZERO_ROWS * TOK_TILES), :], zero_sem)
                cp.wait() if wait else cp.start()
                return carry

            lax.fori_loop(0, tail_ref[1, N_EXPERTS], pad_chunk, 0)

    def issue(t, carry):
        for k in range(TOP_K):
            _row_copy(hs_ref, t, xs_ref, dest_ref[k, t], sem).start()
        return carry

    lax.fori_loop(0, tm, issue, 0, unroll=8)
    n = tm * TOK_TILES
    for k in range(TOP_K):
        pltpu.make_async_copy(hs_ref, xs_ref.at[pl.ds(0, n), :], sem).wait()


def moe_dispatch(hs, dest, tail, n_rows, tm_pref=512):
    T = dest.shape[1]
    tm = _tile(T, tm_pref)
    grid_spec = pltpu.PrefetchScalarGridSpec(
        num_scalar_prefetch=1, grid=(T // tm,),
        in_specs=[pl.BlockSpec((TOP_K, tm), lambda i, tl: (0, i), memory_space=pltpu.SMEM),
                  pl.BlockSpec((tm * TOK_TILES, LANES), lambda i, tl: (i, 0))],
        out_specs=pl.BlockSpec(memory_space=pl.ANY),
        scratch_shapes=[pltpu.VMEM((ZERO_ROWS * TOK_TILES, LANES), F32), pltpu.SemaphoreType.DMA(()),
                        pltpu.SemaphoreType.DMA(())])
    return pl.pallas_call(
        functools.partial(_dispatch_kernel, tm), grid_spec=grid_spec,
        out_shape=jax.ShapeDtypeStruct((n_rows * TOK_TILES, LANES), F32),
        compiler_params=_params("arbitrary"), name="moe_dispatch",
    )(tail, dest, hs)


def _expert_kernel(tm, be_ref, nu_ref, xs_ref, wg_ref, wu_ref, wd_ref, ys_ref, h_ref, acc_ref):
    b = pl.program_id(0)
    j = pl.program_id(1)
    last = pl.num_programs(1) - 1
    used = b < nu_ref[0]

    @pl.when(used & (j == 0))
    def _():
        for s in range(TOK_TILES):
            h_ref[:, s * LANES:(s + 1) * LANES] = xs_ref[pl.ds(s, tm, stride=TOK_TILES), :].astype(BF16)
        acc_ref[...] = jnp.zeros_like(acc_ref)

    @pl.when(used)
    def _():
        _swiglu_step(h_ref[...], wg_ref, wu_ref, wd_ref, acc_ref)

    @pl.when(used & (j == last))
    def _():
        for s in range(TOK_TILES):
            ys_ref[pl.ds(s, tm, stride=TOK_TILES), :] = acc_ref[:, s * LANES:(s + 1) * LANES]

    @pl.when(jnp.logical_not(used) & (j == last))
    def _():
        ys_ref[...] = jnp.zeros_like(ys_ref)


def moe_experts(xs, blk_expert, n_used, w_gate, w_up, w_down, tm):
    E, D, F = w_gate.shape
    nb = xs.shape[0] // (tm * TOK_TILES)
    tf = _tile(F, FF_TILE)
    nj = F // tf
    jj = lambda b, j, be, nu: jnp.where(b < nu[0], j, nj - 1)
    grid_spec = pltpu.PrefetchScalarGridSpec(
        num_scalar_prefetch=2, grid=(nb, nj),
        in_specs=[pl.BlockSpec((tm * TOK_TILES, LANES), lambda b, j, be, nu: (jnp.minimum(b, nu[0] - 1), 0)),
                  pl.BlockSpec((None, D, tf), lambda b, j, be, nu: (be[b], 0, jj(b, j, be, nu))),
                  pl.BlockSpec((None, D, tf), lambda b, j, be, nu: (be[b], 0, jj(b, j, be, nu))),
                  pl.BlockSpec((None, tf, D), lambda b, j, be, nu: (be[b], jj(b, j, be, nu), 0))],
        out_specs=pl.BlockSpec((tm * TOK_TILES, LANES), lambda b, j, be, nu: (b, 0)),
        scratch_shapes=[pltpu.VMEM((tm, D), BF16), pltpu.VMEM((tm, D), F32)])
    return pl.pallas_call(
        functools.partial(_expert_kernel, tm), grid_spec=grid_spec,
        out_shape=jax.ShapeDtypeStruct(xs.shape, F32),
        compiler_params=_params("arbitrary", "arbitrary"), name="moe_experts",
    )(blk_expert, n_used, xs, w_gate, w_up, w_down)


def _combine_kernel(tm, dest_ref, ys_ref, mf_ref, x_ref, g_ref, o_ref, buf_ref, y_ref, sem):
    def issue(t, carry):
        for k in range(TOP_K):
            pltpu.make_async_copy(ys_ref.at[pl.ds(dest_ref[k, t] * TOK_TILES, TOK_TILES), :],
                                  buf_ref.at[k, pl.ds(t * TOK_TILES, TOK_TILES), :], sem).start()
        return carry

    lax.fori_loop(0, tm, issue, 0, unroll=8)
    n = tm * TOK_TILES
    for k in range(TOP_K):
        pltpu.make_async_copy(ys_ref.at[pl.ds(0, n), :], buf_ref.at[k], sem).wait()
    mf = mf_ref[...]
    g1, g2 = mf[:, 0:1], mf[:, 1:2]
    for s in range(TOK_TILES):
        y_ref[:, s * LANES:(s + 1) * LANES] = (
            buf_ref[0, pl.ds(s, tm, stride=TOK_TILES), :] * g1 +
            buf_ref[1, pl.ds(s, tm, stride=TOK_TILES), :] * g2)
    o_ref[...] = _rmsnorm(x_ref[...] + y_ref[...], g_ref[...])


def moe_combine(ys, dest, meta_f, x, g_final, tm_pref=512):
    T, D = x.shape
    tm = _tile(T, tm_pref)
    return pl.pallas_call(
        functools.partial(_combine_kernel, tm), grid=(T // tm,),
        in_specs=[pl.BlockSpec((TOP_K, tm), lambda i: (0, i), memory_space=pltpu.SMEM),
                  pl.BlockSpec(memory_space=pl.ANY),
                  pl.BlockSpec((tm, LANES), lambda i: (i, 0)),
                  pl.BlockSpec((tm, D), lambda i: (i, 0)),
                  pl.BlockSpec((1, D), lambda i: (0, 0))],
        out_specs=pl.BlockSpec((tm, D), lambda i: (i, 0)),
        out_shape=jax.ShapeDtypeStruct((T, D), F32),
        scratch_shapes=[pltpu.VMEM((TOP_K, tm * TOK_TILES, LANES), F32), pltpu.VMEM((tm, D), F32),
                        pltpu.SemaphoreType.DMA(())],
        compiler_params=_params("arbitrary"), name="moe_combine",
    )(dest, ys, meta_f, x, g_final.reshape(1, D))


def moe_layer(x, g_ffn, g_final, w_router, w_gate, w_up, w_down):
    T = x.shape[0]
    tm = MOE_ROWS_LARGE if T * TOP_K >= MOE_ROWS_LARGE * N_EXPERTS * 8 else MOE_ROWS_SMALL
    hs, meta_i, meta_f, cnt = moe_router(x, g_ffn, w_router)
    counts = cnt[0, :N_EXPERTS].astype(I32)
    padded = (counts + tm - 1) // tm * tm
    pad_end = jnp.cumsum(padded)
    pad_start = pad_end - padded
    idx = meta_i[:, 0:TOP_K].T
    rank = meta_i[:, TOP_K:2 * TOP_K].T
    first_row = jnp.sum(jnp.where(idx[..., None] == jnp.arange(N_EXPERTS, dtype=I32), pad_start, 0), axis=-1)
    dest = first_row + rank
    nb = (T * TOP_K) // tm + N_EXPERTS
    n_used = (pad_end[-1] // tm).astype(I32)
    blk_row = jnp.minimum(jnp.arange(nb, dtype=I32), n_used - 1) * tm
    blk_expert = jnp.minimum(jnp.searchsorted(pad_end, blk_row, side='right'), N_EXPERTS - 1).astype(I32)
    tail = jnp.stack([jnp.append(pad_start + counts, pad_end[-1]),
                      jnp.append(padded - counts, (nb * tm - pad_end[-1]) // ZERO_ROWS)]).astype(I32)
    xs = moe_dispatch(hs, dest, tail, nb * tm)
    ys = moe_experts(xs, blk_expert, n_used.reshape(1), w_gate, w_up, w_down, tm)
    return moe_combine(ys, dest, meta_f, x, g_final)


def _prep_weights(g_mix, g_mem, g_ffn, g_final, a_w_in, a_conv, a_log, a_dt_bias, a_o_norm,
                  a_mem_kv, a_w_out, b_w_in, b_sink, b_mem_kv, b_w_out, f_w_gate, f_w_up, f_w_down,
                  e_router, e_w_gate, e_w_up, e_w_down):
    s1 = DN_QKV
    s2 = s1 + DN_V
    s3 = s2 + 4 * DN_HEADS
    w = a_w_in[0]
    gates_w = jnp.zeros((D_MODEL, LANES), F32).at[:, :4 * DN_HEADS].set(w[:, s2:s3])
    wb = b_w_in[0]
    return dict(
        g_mix=g_mix, g_mem=g_mem, g_ffn=g_ffn, g_final=g_final,
        a_in=[w[:, :s1].astype(BF16), w[:, s1:s2].astype(BF16), w[:, s3:].astype(BF16), gates_w],
        a_conv=a_conv[0], a_log=a_log[0], a_dt_bias=a_dt_bias[0], a_o_norm=a_o_norm[0],
        a_mem_kv=a_mem_kv[0].astype(BF16), a_w_out=a_w_out[0].astype(BF16),
        b_in=[wb[:, :WA_Q].astype(BF16), wb[:, WA_Q:WA_Q + WA_KV].astype(BF16),
              wb[:, WA_Q + WA_KV:WA_Q + 2 * WA_KV].astype(BF16), wb[:, WA_Q + 2 * WA_KV:].astype(BF16)],
        b_sink=b_sink[0], b_mem_kv=b_mem_kv[0].astype(BF16), b_w_out=b_w_out[0].astype(BF16),
        f_w_gate=f_w_gate[0].astype(BF16), f_w_up=f_w_up[0].astype(BF16), f_w_down=f_w_down[0].astype(BF16),
        e_router=e_router[0], e_w_gate=e_w_gate[0].astype(BF16), e_w_up=e_w_up[0].astype(BF16),
        e_w_down=e_w_down[0].astype(BF16))


def _trunk(x, mem, p):
    B, L, D = x.shape
    M = mem.shape[1]
    T = B * L
    x2 = x.reshape(T, D)
    mem2 = mem.reshape(B * M, D)

    qkv, z, qc, gates = norm_proj(x2, p['g_mix'][0], p['a_in'], [BF16, BF16, BF16, F32],
                                  scales=(None, None, MX_HEAD_DIM ** -0.5 * LOG2_E, None))
    (kv,) = norm_proj(mem2, p['g_mem'][0], [p['a_mem_kv']], [BF16])
    q, k, v, pack, packt = dn_prep(qkv.reshape(B, L, DN_QKV), gates.reshape(B, L, LANES),
                                   p['a_conv'], p['a_log'], p['a_dt_bias'])
    o_f, o_b = delta_rule(q, k, v, pack, packt)
    xo = mem_attention(qc.reshape(B, L, MX_W), kv.reshape(B, M, 2 * MX_W))
    x1 = out_proj_a(o_f, o_b, z.reshape(B, L, DN_V), xo, x, p['a_o_norm'], p['a_w_out'])
    x1 = ffn(x1.reshape(T, D), p['g_ffn'][0], p['f_w_gate'], p['f_w_up'], p['f_w_down'])

    cos, sin_signed = _rope_tables(L)
    q, k, v, qc = norm_proj(x1, p['g_mix'][1], p['b_in'], [BF16, BF16, BF16, BF16],
                            scales=(WA_HEAD_DIM ** -0.5 * LOG2_E, None, None, MX_HEAD_DIM ** -0.5 * LOG2_E),
                            rope=(cos, sin_signed, (True, True, False, False)))
    (kv,) = norm_proj(mem2, p['g_mem'][1], [p['b_mem_kv']], [BF16])
    o = window_attention(q.reshape(B, L, WA_Q), k.reshape(B, L, WA_KV), v.reshape(B, L, WA_KV),
                         p['b_sink'])
    xo = mem_attention(qc.reshape(B, L, MX_W), kv.reshape(B, M, 2 * MX_W))
    x2 = out_proj_b(o, xo, x1.reshape(B, L, D), p['b_w_out'])
    y = moe_layer(x2.reshape(T, D), p['g_ffn'][1], p['g_final'], p['e_router'],
                  p['e_w_gate'], p['e_w_up'], p['e_w_down'])
    return y.reshape(B, L, D)


def kernel(x_prompt, x_sample, mem_prompt, mem_sample, g_mix, g_mem, g_ffn, g_final, a_w_in, a_conv,
           a_log, a_dt_bias, a_o_norm, a_mem_kv, a_w_out, b_w_in, b_sink, b_mem_kv, b_w_out,
           f_w_gate, f_w_up, f_w_down, e_router, e_w_gate, e_w_up, e_w_down):
    p = _prep_weights(g_mix, g_mem, g_ffn, g_final, a_w_in, a_conv, a_log, a_dt_bias, a_o_norm,
                      a_mem_kv, a_w_out, b_w_in, b_sink, b_mem_kv, b_w_out, f_w_gate, f_w_up, f_w_down,
                      e_router, e_w_gate, e_w_up, e_w_down)
    return (_trunk(x_prompt, mem_prompt, p), _trunk(x_sample, mem_sample, p))
```

```python
import functools
import math

import jax
import jax.numpy as jnp
from jax import lax
from jax.experimental import pallas as pl
from jax.experimental.pallas import tpu as pltpu

F32 = jnp.float32
BF16 = jnp.bfloat16
I32 = jnp.int32

EPS = 1e-6
LOG2_E = math.log2(math.e)
D_MODEL = 1024
LANES = 128
SUBLANES = 8
TOK_TILES = D_MODEL // LANES
VMEM_LIMIT_BYTES = 56 * 1024 * 1024

DN_HEADS = 6
DN_D = 128
DN_CONV = 5
DN_CHUNK = 64
DN_QKV = 3 * DN_HEADS * DN_D
DN_V = DN_HEADS * DN_D
DN_TILE = 256
CONV_HALO = 16

WA_Q_HEADS = 12
WA_KV_HEADS = 4
WA_GROUP = WA_Q_HEADS // WA_KV_HEADS
WA_HEAD_DIM = 64
WA_BLOCK = 128
WA_QBLOCK = 256
WA_Q = WA_Q_HEADS * WA_HEAD_DIM
WA_KV = WA_KV_HEADS * WA_HEAD_DIM
ROPE_THETA = 10000.0

MX_HEADS = 4
MX_HEAD_DIM = 64
MX_W = MX_HEADS * MX_HEAD_DIM

D_FF = 3584
N_EXPERTS = 8
TOP_K = 2
FF_TILE = 512
MOE_TILE = 512
MOE_ROWS_SMALL = 512
MOE_ROWS_LARGE = 1024
ZERO_ROWS = 64
SEG_CHUNK = 8


def _params(*sem):
    return pltpu.CompilerParams(dimension_semantics=sem, vmem_limit_bytes=VMEM_LIMIT_BYTES)


def _tile(n, pref):
    t = min(n, pref)
    assert n % t == 0, (n, pref)
    return t


def _dot(a, b):
    return jnp.dot(a, b, preferred_element_type=F32)


def _dot_nt(a, b):
    return lax.dot_general(a, b, (((1,), (1,)), ((), ())), preferred_element_type=F32)


def _dot_tn(a, b):
    return lax.dot_general(a, b, (((0,), (0,)), ((), ())), preferred_element_type=F32)


def _split3(x):
    hi = x.astype(BF16)
    r1 = x - hi.astype(F32)
    mid = r1.astype(BF16)
    lo = (r1 - mid.astype(F32)).astype(BF16)
    return hi, mid, lo


def _dot_x3(a, b):
    a_hi = a.astype(BF16)
    a_lo = (a - a_hi.astype(F32)).astype(BF16)
    b_hi = b.astype(BF16)
    b_lo = (b - b_hi.astype(F32)).astype(BF16)
    return _dot(a_hi, b_hi) + _dot(a_lo, b_hi) + _dot(a_hi, b_lo)


def _rmsnorm(x, g):
    ms = jnp.mean(x * x, axis=-1, keepdims=True)
    return x * lax.rsqrt(ms + EPS) * g


def _silu(x):
    return x * jax.nn.sigmoid(x)


def _rope(x, cos, sin_signed):
    n, w = x.shape
    lane = lax.broadcasted_iota(I32, (n, LANES), 1)
    first_half = (lane % WA_HEAD_DIM) < (WA_HEAD_DIM // 2)
    outs = []
    for gidx in range(w // LANES):
        xs = x[:, gidx * LANES:(gidx + 1) * LANES]
        rot = jnp.where(first_half, pltpu.roll(xs, LANES - WA_HEAD_DIM // 2, 1),
                        pltpu.roll(xs, WA_HEAD_DIM // 2, 1))
        outs.append(xs * cos + rot * sin_signed)
    return jnp.concatenate(outs, axis=1)


def _norm_proj_kernel(n_out, scales, rope_flags, x_ref, g_ref, *refs):
    if rope_flags is not None:
        cos_ref, sin_ref, refs = refs[0], refs[1], refs[2:]
    w_refs, o_refs = refs[:n_out], refs[n_out:]
    hf = _rmsnorm(x_ref[...], g_ref[...])
    hb = hf.astype(BF16)
    for idx, (w_ref, o_ref) in enumerate(zip(w_refs, o_refs)):
        if w_ref.dtype == F32:
            y = _dot_x3(hf, w_ref[...])
        else:
            y = _dot(hb, w_ref[...])
        if rope_flags is not None and rope_flags[idx]:
            y = _rope(y, cos_ref[...], sin_ref[...])
        if scales is not None and scales[idx] is not None:
            y = y * scales[idx]
        o_ref[...] = y.astype(o_ref.dtype)


def norm_proj(x, g, weights, out_dtypes, scales=None, rope=None, tm_pref=512):
    T, D = x.shape
    tm = _tile(T, tm_pref)
    n_out = len(weights)
    args = [x, g.reshape(1, D)]
    extra_specs = []
    flags = None
    if rope is not None:
        cos, sin_signed, flags = rope
        tm = _tile(cos.shape[0], tm)
        per_seq = cos.shape[0] // tm
        extra_specs = [pl.BlockSpec((tm, LANES), lambda i: (i % per_seq, 0))] * 2
        args += [cos, sin_signed]
    in_specs = [pl.BlockSpec((tm, D), lambda i: (i, 0)), pl.BlockSpec((1, D), lambda i: (0, 0))]
    in_specs += extra_specs + [pl.BlockSpec(w.shape, lambda i: (0, 0)) for w in weights]
    out_specs = [pl.BlockSpec((tm, w.shape[1]), lambda i: (i, 0)) for w in weights]
    out_shape = [jax.ShapeDtypeStruct((T, w.shape[1]), dt) for w, dt in zip(weights, out_dtypes)]
    return pl.pallas_call(
        functools.partial(_norm_proj_kernel, n_out, scales, flags),
        grid=(T // tm,), in_specs=in_specs, out_specs=out_specs, out_shape=out_shape,
        compiler_params=_params("parallel"), name="norm_proj",
    )(*args, *weights)


def _dn_prep_kernel(tl, cur_ref, prev_ref, next_ref, gates_ref, cw_ref, alog_ref, dtb_ref, shift_ref,
                    q_ref, k_ref, v_ref, pack_ref, packt_ref):
    i = pl.program_id(1)
    nt = pl.num_programs(1)
    pad = (DN_CONV - 1) // 2
    taps = [j for j in range(DN_CONV) if j != pad]
    p8 = prev_ref[CONV_HALO - SUBLANES:, :].astype(F32) * (i > 0).astype(F32)
    n8 = next_ref[0:SUBLANES, :].astype(F32) * (i < nt - 1).astype(F32)
    row8 = lax.broadcasted_iota(I32, (SUBLANES, 1), 0)
    first = None
    last = None
    for j in range(DN_CONV):
        off = j - pad
        if off < 0:
            t = jnp.where(row8 < -off, pltpu.roll(p8, -off, 0), 0.0) * cw_ref[j:j + 1, :]
            first = t if first is None else first + t
        elif off > 0:
            t = jnp.where(row8 >= SUBLANES - off, pltpu.roll(n8, SUBLANES - off, 0), 0.0) * cw_ref[j:j + 1, :]
            last = t if last is None else last + t
    mxu_cols = 2 * LANES
    moved = None
    for cg in range(DN_QKV // LANES):
        sl = slice(cg * LANES, (cg + 1) * LANES)
        if cg % 2 == 0:
            x16 = cur_ref[:, cg * LANES:cg * LANES + mxu_cols]
            moved = {j: _dot(shift_ref[n], x16) for n, j in enumerate(taps)}
            moved[pad] = x16.astype(F32)
        half = slice((cg % 2) * LANES, (cg % 2 + 1) * LANES)
        acc = None
        for j in range(DN_CONV):
            term = moved[j][:, half] * cw_ref[j:j + 1, sl]
            acc = term if acc is None else acc + term
        acc = acc + jnp.concatenate(
            [first[:, sl], jnp.zeros((tl - 2 * SUBLANES, LANES), F32), last[:, sl]], axis=0)
        y = _silu(acc)
        if cg < DN_HEADS:
            y = y * lax.rsqrt(jnp.sum(y * y, axis=-1, keepdims=True) + EPS) * (DN_D ** -0.5)
            q_ref[cg] = y.astype(q_ref.dtype)
        elif cg < 2 * DN_HEADS:
            y = y * lax.rsqrt(jnp.sum(y * y, axis=-1, keepdims=True) + EPS)
            k_ref[cg - DN_HEADS] = y.astype(k_ref.dtype)
        else:
            v_ref[cg - 2 * DN_HEADS] = y.astype(v_ref.dtype)

    gt = gates_ref[...]
    lane = lax.broadcasted_iota(I32, gt.shape, 1)
    beta = jax.nn.sigmoid(gt)
    xg = gt + dtb_ref[...]
    softplus = jnp.maximum(xg, 0.0) + jnp.log1p(jnp.exp(-jnp.abs(xg)))
    g = -jnp.exp(alog_ref[...]) * softplus
    g = jnp.where((lane >= 2 * DN_HEADS) & (lane < 4 * DN_HEADS), g, 0.0)
    r = lax.broadcasted_iota(I32, (tl, tl), 0)
    c = lax.broadcasted_iota(I32, (tl, tl), 1)
    same = (r // DN_CHUNK) == (c // DN_CHUNK)
    m_fwd = (same & (r >= c)).astype(BF16)
    m_bwd = (same & (r <= c)).astype(BF16)
    m_all = same.astype(BF16)
    hi, mid, lo = _split3(g)
    g_fwd = _dot(m_fwd, hi) + _dot(m_fwd, mid) + _dot(m_fwd, lo)
    g_bwd = _dot(m_bwd, hi) + _dot(m_bwd, mid) + _dot(m_bwd, lo)
    g_tot = pltpu.roll(_dot(m_all, hi) + _dot(m_all, mid) + _dot(m_all, lo), 2 * DN_HEADS, 1)
    pack = jnp.where(lane < 2 * DN_HEADS, beta,
                     jnp.where(lane < 3 * DN_HEADS, g_fwd,
                               jnp.where(lane < 4 * DN_HEADS, g_bwd,
                                         jnp.where(lane < 6 * DN_HEADS, g_tot, 0.0))))
    pack_ref[...] = pack
    packt_ref[...] = pack.T


def dn_prep(qkv, gates, conv_w, a_log, dt_bias):
    B, L, _ = qkv.shape
    tl = _tile(L, DN_TILE)
    nt = L // tl
    hb = tl // CONV_HALO
    nhalo = L // CONV_HALO
    cw = jnp.zeros((SUBLANES, DN_QKV), F32).at[:DN_CONV].set(conv_w)
    alog = jnp.zeros((1, LANES), F32).at[0, 2 * DN_HEADS:4 * DN_HEADS].set(a_log.reshape(-1))
    dtb = jnp.zeros((1, LANES), F32).at[0, 2 * DN_HEADS:4 * DN_HEADS].set(dt_bias.reshape(-1))
    pad = (DN_CONV - 1) // 2
    t_idx = jnp.arange(tl)
    shift = jnp.stack([(t_idx[None, :] == t_idx[:, None] + (j - pad)).astype(BF16)
                       for j in range(DN_CONV) if j != pad])
    hm = jax.ShapeDtypeStruct((B, DN_HEADS, L, DN_D), BF16)
    hm_spec = pl.BlockSpec((None, DN_HEADS, tl, DN_D), lambda b, i: (b, 0, i, 0))
    return pl.pallas_call(
        functools.partial(_dn_prep_kernel, tl),
        grid=(B, nt),
        in_specs=[
            pl.BlockSpec((None, tl, DN_QKV), lambda b, i: (b, i, 0)),
            pl.BlockSpec((None, CONV_HALO, DN_QKV), lambda b, i: (b, jnp.maximum(i * hb - 1, 0), 0)),
            pl.BlockSpec((None, CONV_HALO, DN_QKV), lambda b, i: (b, jnp.minimum((i + 1) * hb, nhalo - 1), 0)),
            pl.BlockSpec((None, tl, LANES), lambda b, i: (b, i, 0)),
            pl.BlockSpec((SUBLANES, DN_QKV), lambda b, i: (0, 0)),
            pl.BlockSpec((1, LANES), lambda b, i: (0, 0)),
            pl.BlockSpec((1, LANES), lambda b, i: (0, 0)),
            pl.BlockSpec(shift.shape, lambda b, i: (0, 0, 0)),
        ],
        out_specs=[hm_spec, hm_spec, hm_spec,
                   pl.BlockSpec((None, tl, LANES), lambda b, i: (b, i, 0)),
                   pl.BlockSpec((None, LANES, tl), lambda b, i: (b, 0, i))],
        out_shape=[hm, hm, hm, jax.ShapeDtypeStruct((B, L, LANES), F32),
                   jax.ShapeDtypeStruct((B, LANES, L), F32)],
        compiler_params=_params("parallel", "parallel"), name="dn_prep",
    )(qkv, qkv, qkv, gates, cw, alog, dtb, shift)


def _dn_tile_stage(units, tl, refs_f, refs_b, masks_f, masks_b, u_ref, w_ref, qd_ref, kd_ref, aqk_ref):
    n = len(units)
    nchunk = tl // DN_CHUNK
    refs = [refs_f if d == 0 else refs_b for d, _ in units]
    masks = [masks_f if d == 0 else masks_b for d, _ in units]
    col = []
    for (d, h), (q_ref, k_ref, v_ref, pack_ref, packt_ref) in zip(units, refs):
        pick = lambda li: pack_ref[:, li:li + 1]
        li_g = (2 + d) * DN_HEADS + h
        col.append((pick(d * DN_HEADS + h), pick(li_g), pick((4 + d) * DN_HEADS + h),
                    packt_ref[li_g:li_g + 1, :]))
    q16 = [r[0][h] for (d, h), r in zip(units, refs)]
    k16 = [r[1][h] for (d, h), r in zip(units, refs)]
    kf = [x.astype(F32) for x in k16]
    kb = [kf[i] * col[i][0] for i in range(n)]
    decay = [jnp.exp(jnp.where(masks[i][0], col[i][1] - col[i][3], -jnp.inf)) for i in range(n)]
    kk = [_dot_nt(kb[i].astype(BF16), k16[i]) for i in range(n)]
    qk = [_dot_nt(q16[i], k16[i]) for i in range(n)]
    a = [jnp.where(masks[i][1], kk[i] * decay[i], 0.0) for i in range(n)]
    for i, (d, h) in enumerate(units):
        a_qk = (qk[i] * decay[i]).astype(BF16)
        for c in range(nchunk):
            rs = slice(c * DN_CHUNK, (c + 1) * DN_CHUNK)
            aqk_ref[d * DN_HEADS + h, rs, :] = a_qk[rs, rs]
    same, eye_wide = masks[0][2], masks[0][3]

    def wide(x):
        out = x[0:DN_CHUNK]
        for c in range(1, nchunk):
            out = out + x[c * DN_CHUNK:(c + 1) * DN_CHUNK]
        return out

    def block_diag(x16):
        return jnp.where(same, jnp.concatenate([x16] * nchunk, axis=0), jnp.zeros((), BF16))

    t_w = [eye_wide - wide(x) for x in a]
    p_w16 = [wide(x).astype(BF16) for x in a]
    p_bd16 = [x.astype(BF16) for x in a]
    for it in range(int(math.log2(DN_CHUNK)) - 1):
        p_w16 = [_dot(p_w16[i], p_bd16[i]).astype(BF16) for i in range(n)]
        p_bd16 = [block_diag(x) for x in p_w16]
        t_w = [t_w[i] + _dot(t_w[i].astype(BF16), p_bd16[i]) for i in range(n)]
    t_bd16 = [block_diag(x.astype(BF16)) for x in t_w]
    e_g = [jnp.exp(col[i][1]) for i in range(n)]
    rhs = [jnp.concatenate([refs[i][2][units[i][1]].astype(F32) * col[i][0], kb[i] * e_g[i]],
                           axis=1).astype(BF16) for i in range(n)]
    uw = [_dot(t_bd16[i], rhs[i]) for i in range(n)]
    for i, (d, h) in enumerate(units):
        ui = d * DN_HEADS + h
        u_ref[ui] = uw[i][:, :DN_D]
        w_ref[ui] = uw[i][:, DN_D:].astype(BF16)
        qd_ref[ui] = (q16[i].astype(F32) * e_g[i]).astype(BF16)
        kd_ref[ui] = (kf[i] * jnp.exp(col[i][2] - col[i][1])).astype(BF16)


def _dn_scan_stage(tl, ptf_ref, ptb_ref, of_ref, ob_ref, s_ref, u_ref, w_ref, qd_ref, kd_ref, aqk_ref):
    nchunk = tl // DN_CHUNK
    chains = [(d, h) for h in range(DN_HEADS) for d in range(2)]
    ids = [d * DN_HEADS + h for d, h in chains]
    for step in range(nchunk):
        rows = []
        for d, h in chains:
            c = step if d == 0 else nchunk - 1 - step
            rows.append(slice(c * DN_CHUNK, (c + 1) * DN_CHUNK))
        s = [s_ref[ui] for ui in ids]
        sb = [x.astype(BF16) for x in s]
        v_new = [(u_ref[ui, rs, :] - _dot(w_ref[ui, rs, :], b)).astype(BF16)
                 for ui, rs, b in zip(ids, rows, sb)]
        o = [_dot(qd_ref[ui, rs, :], b) + _dot(aqk_ref[ui, rs, :], vn)
             for ui, rs, b, vn in zip(ids, rows, sb, v_new)]
        for (d, h), ui, rs, x, vn, oo in zip(chains, ids, rows, s, v_new, o):
            pt_ref = ptf_ref if d == 0 else ptb_ref
            li_tot = (4 + d) * DN_HEADS + h
            tot = pt_ref[li_tot:li_tot + 1, rs.start:rs.start + 1]
            s_ref[ui] = x * jnp.exp(tot) + _dot_tn(kd_ref[ui, rs, :], vn)
            (of_ref if d == 0 else ob_ref)[h, rs, :] = oo.astype(of_ref.dtype)


def _dn_kernel(tl, qf_ref, kf_ref, vf_ref, pf_ref, ptf_ref, qb_ref, kb_ref, vb_ref, pb_ref, ptb_ref,
               of_ref, ob_ref, s_ref, u_ref, w_ref, qd_ref, kd_ref, aqk_ref):
    @pl.when(pl.program_id(1) == 0)
    def _():
        s_ref[...] = jnp.zeros_like(s_ref)

    r = lax.broadcasted_iota(I32, (tl, tl), 0)
    c = lax.broadcasted_iota(I32, (tl, tl), 1)
    same = (r // DN_CHUNK) == (c // DN_CHUNK)
    eye_wide = (lax.broadcasted_iota(I32, (DN_CHUNK, tl), 0) ==
                lax.broadcasted_iota(I32, (DN_CHUNK, tl), 1) % DN_CHUNK).astype(F32)
    masks_f = (same & (r >= c), same & (r > c), same, eye_wide)
    masks_b = (same & (r <= c), same & (r < c), same, eye_wide)
    refs_f = (qf_ref, kf_ref, vf_ref, pf_ref, ptf_ref)
    refs_b = (qb_ref, kb_ref, vb_ref, pb_ref, ptb_ref)

    units = [(d, h) for h in range(DN_HEADS) for d in range(2)]
    _dn_tile_stage(units, tl, refs_f, refs_b, masks_f, masks_b, u_ref, w_ref, qd_ref, kd_ref, aqk_ref)
    _dn_scan_stage(tl, ptf_ref, ptb_ref, of_ref, ob_ref, s_ref, u_ref, w_ref, qd_ref, kd_ref, aqk_ref)


def delta_rule(q, k, v, pack, packt):
    B, H, L, _ = q.shape
    tl = _tile(L, DN_TILE)
    nt = L // tl
    fwd = lambda b, i: (b, 0, i, 0)
    bwd = lambda b, i: (b, 0, nt - 1 - i, 0)
    hm_f = pl.BlockSpec((None, H, tl, DN_D), fwd)
    hm_b = pl.BlockSpec((None, H, tl, DN_D), bwd)
    pk_f = pl.BlockSpec((None, tl, LANES), lambda b, i: (b, i, 0))
    pk_b = pl.BlockSpec((None, tl, LANES), lambda b, i: (b, nt - 1 - i, 0))
    pt_f = pl.BlockSpec((None, LANES, tl), lambda b, i: (b, 0, i))
    pt_b = pl.BlockSpec((None, LANES, tl), lambda b, i: (b, 0, nt - 1 - i))
    o = jax.ShapeDtypeStruct((B, H, L, DN_D), BF16)
    return pl.pallas_call(
        functools.partial(_dn_kernel, tl),
        grid=(B, nt),
        in_specs=[hm_f, hm_f, hm_f, pk_f, pt_f, hm_b, hm_b, hm_b, pk_b, pt_b],
        out_specs=[hm_f, hm_b], out_shape=[o, o],
        scratch_shapes=[pltpu.VMEM((2 * H, DN_D, DN_D), F32),
                        pltpu.VMEM((2 * H, tl, DN_D), F32),
                        pltpu.VMEM((2 * H, tl, DN_D), BF16),
                        pltpu.VMEM((2 * H, tl, DN_D), BF16),
                        pltpu.VMEM((2 * H, tl, DN_D), BF16),
                        pltpu.VMEM((2 * H, tl, DN_CHUNK), BF16)],
        compiler_params=_params("parallel", "arbitrary"), name="delta_rule",
    )(q, k, v, pack, packt, q, k, v, pack, packt)


def _mem_attn_kernel(q_ref, kv_ref, o_ref):
    q = q_ref[...]
    kv = kv_ref[...]
    outs = []
    for h in range(MX_HEADS):
        hs = slice(h * MX_HEAD_DIM, (h + 1) * MX_HEAD_DIM)
        vs = slice(MX_W + h * MX_HEAD_DIM, MX_W + (h + 1) * MX_HEAD_DIM)
        s = _dot_nt(kv[:, hs], q[:, hs])
        m = jnp.max(s, axis=0, keepdims=True)
        p = jnp.exp2(s - m)
        den = jnp.sum(p, axis=0, keepdims=True)
        outs.append((_dot_tn(kv[:, vs], p.astype(BF16)) / den).T)
    o_ref[...] = jnp.concatenate(outs, axis=1).astype(o_ref.dtype)


def mem_attention(qc, kv, tl_pref=512):
    B, L, _ = qc.shape
    M = kv.shape[1]
    tl = _tile(L, tl_pref)
    return pl.pallas_call(
        _mem_attn_kernel, grid=(B, L // tl),
        in_specs=[pl.BlockSpec((None, tl, MX_W), lambda b, i: (b, i, 0)),
                  pl.BlockSpec((None, M, 2 * MX_W), lambda b, i: (b, 0, 0))],
        out_specs=pl.BlockSpec((None, tl, MX_W), lambda b, i: (b, i, 0)),
        out_shape=jax.ShapeDtypeStruct((B, L, MX_W), BF16),
        compiler_params=_params("parallel", "parallel"), name="mem_attention",
    )(qc, kv)


def _out_proj_a_kernel(of_ref, ob_ref, z_ref, xo_ref, x_ref, on_ref, w_ref, o_ref):
    parts = []
    z = z_ref[...].astype(F32)
    for h in range(DN_HEADS):
        o = of_ref[h].astype(F32) + ob_ref[h].astype(F32)
        y = _rmsnorm(o, on_ref[...]) * _silu(z[:, h * DN_D:(h + 1) * DN_D])
        parts.append(y.astype(BF16))
    parts.append(xo_ref[...])
    mixed = jnp.concatenate(parts, axis=1)
    o_ref[...] = x_ref[...] + _dot(mixed, w_ref[...])


def out_proj_a(o_f, o_b, z, xo, x, o_norm, w_out, tm_pref=512):
    B, H, L, _ = o_f.shape
    D = x.shape[-1]
    tm = _tile(L, tm_pref)
    hm = pl.BlockSpec((None, H, tm, DN_D), lambda b, i: (b, 0, i, 0))
    row = lambda w: pl.BlockSpec((None, tm, w), lambda b, i: (b, i, 0))
    return pl.pallas_call(
        _out_proj_a_kernel, grid=(B, L // tm),
        in_specs=[hm, hm, row(DN_V), row(MX_W), row(D),
                  pl.BlockSpec((1, DN_D), lambda b, i: (0, 0)),
                  pl.BlockSpec(w_out.shape, lambda b, i: (0, 0))],
        out_specs=row(D), out_shape=jax.ShapeDtypeStruct((B, L, D), F32),
        compiler_params=_params("parallel", "parallel"), name="out_proj_a",
    )(o_f, o_b, z, xo, x, o_norm.reshape(1, DN_D), w_out)


def _out_proj_b_kernel(o_ref, xo_ref, x_ref, w_ref, out_ref):
    mixed = jnp.concatenate([o_ref[...], xo_ref[...]], axis=1)
    out_ref[...] = x_ref[...] + _dot(mixed, w_ref[...])


def out_proj_b(o, xo, x, w_out, tm_pref=512):
    B, L, D = x.shape
    tm = _tile(L, tm_pref)
    row = lambda w: pl.BlockSpec((None, tm, w), lambda b, i: (b, i, 0))
    return pl.pallas_call(
        _out_proj_b_kernel, grid=(B, L // tm),
        in_specs=[row(WA_Q), row(MX_W), row(D), pl.BlockSpec(w_out.shape, lambda b, i: (0, 0))],
        out_specs=row(D), out_shape=jax.ShapeDtypeStruct((B, L, D), F32),
        compiler_params=_params("parallel", "parallel"), name="out_proj_b",
    )(o, xo, x, w_out)


def _swiglu_step(h, wg_ref, wu_ref, wd_ref, acc_ref):
    gate = _dot(h, wg_ref[...])
    up = _dot(h, wu_ref[...])
    acc_ref[...] += _dot((_silu(gate) * up).astype(BF16), wd_ref[...])


def _ffn_kernel(x_ref, g_ref, wg_ref, wu_ref, wd_ref, o_ref, h_ref, acc_ref):
    j = pl.program_id(1)

    @pl.when(j == 0)
    def _():
        h_ref[...] = _rmsnorm(x_ref[...], g_ref[...]).astype(BF16)
        acc_ref[...] = jnp.zeros_like(acc_ref)

    _swiglu_step(h_ref[...], wg_ref, wu_ref, wd_ref, acc_ref)

    @pl.when(j == pl.num_programs(1) - 1)
    def _():
        o_ref[...] = x_ref[...] + acc_ref[...]


def ffn(x, g, w_gate, w_up, w_down, tm_pref=1024):
    T, D = x.shape
    F = w_gate.shape[1]
    tm = _tile(T, tm_pref)
    tf = _tile(F, FF_TILE)
    return pl.pallas_call(
        _ffn_kernel, grid=(T // tm, F // tf),
        in_specs=[pl.BlockSpec((tm, D), lambda i, j: (i, 0)),
                  pl.BlockSpec((1, D), lambda i, j: (0, 0)),
                  pl.BlockSpec((D, tf), lambda i, j: (0, j)),
                  pl.BlockSpec((D, tf), lambda i, j: (0, j)),
                  pl.BlockSpec((tf, D), lambda i, j: (j, 0))],
        out_specs=pl.BlockSpec((tm, D), lambda i, j: (i, 0)),
        out_shape=jax.ShapeDtypeStruct((T, D), F32),
        scratch_shapes=[pltpu.VMEM((tm, D), BF16), pltpu.VMEM((tm, D), F32)],
        compiler_params=_params("parallel", "arbitrary"), name="ffn",
    )(x, g.reshape(1, D), w_gate, w_up, w_down)


def _win_attn_kernel(sink_ref, q_ref, kp_ref, kc_ref, kn_ref, vp_ref, vc_ref, vn_ref, bias_ref, o_ref):
    i = pl.program_id(1)
    n = pl.num_programs(1)
    qb = q_ref.shape[0]
    q = q_ref[...]
    k = jnp.concatenate([kp_ref[...], kc_ref[...], kn_ref[...]], axis=0)
    v = jnp.concatenate([vp_ref[...], vc_ref[...], vn_ref[...]], axis=0)
    bias = bias_ref[(i == 0).astype(I32) + 2 * (i == n - 1).astype(I32)]
    bias = jnp.concatenate([bias] * WA_GROUP, axis=1)
    cols = WA_GROUP * qb
    grp = lax.broadcasted_iota(I32, (1, cols), 1) // qb
    outs = [None] * WA_Q_HEADS
    for h in range(WA_KV_HEADS):
        hs = slice(h * WA_HEAD_DIM, (h + 1) * WA_HEAD_DIM)
        qs = jnp.concatenate(
            [q[:, (h * WA_GROUP + g) * WA_HEAD_DIM:(h * WA_GROUP + g + 1) * WA_HEAD_DIM]
             for g in range(WA_GROUP)], axis=0)
        sk = jnp.zeros((1, cols), F32)
        for g in range(WA_GROUP):
            sk = jnp.where(grp == g, sink_ref[h * WA_GROUP + g] * LOG2_E, sk)
        s = _dot_nt(k[:, hs], qs) + bias
        m = jnp.maximum(jnp.max(s, axis=0, keepdims=True), sk)
        p = jnp.exp2(s - m)
        den = jnp.sum(p, axis=0, keepdims=True) + jnp.exp2(sk - m)
        o_t = _dot_tn(v[:, hs], p.astype(BF16)) / den
        o = o_t.T
        for g in range(WA_GROUP):
            outs[h * WA_GROUP + g] = o[g * qb:(g + 1) * qb]
    o_ref[...] = jnp.concatenate(outs, axis=1).astype(o_ref.dtype)


def _window_bias(qb):
    nk = qb + 2 * WA_BLOCK
    kj = jnp.arange(nk)[:, None]
    qi = jnp.arange(qb)[None, :]
    band = jnp.abs(kj - WA_BLOCK - qi) <= WA_BLOCK
    not_prev = kj >= WA_BLOCK
    not_next = kj < WA_BLOCK + qb
    variants = [band, band & not_prev, band & not_next, band & not_prev & not_next]
    return jnp.stack([jnp.where(m, 0.0, -jnp.inf).astype(F32) for m in variants])


def window_attention(q, k, v, sink):
    B, L, _ = q.shape
    qb = _tile(L, WA_QBLOCK)
    n = L // qb
    per = qb // WA_BLOCK
    nhalo = L // WA_BLOCK
    prev = lambda b, i, s: (b, jnp.maximum(i * per - 1, 0), 0)
    cur = lambda b, i, s: (b, i, 0)
    nxt = lambda b, i, s: (b, jnp.minimum((i + 1) * per, nhalo - 1), 0)
    halo = lambda f: pl.BlockSpec((None, WA_BLOCK, WA_KV), f)
    body = pl.BlockSpec((None, qb, WA_KV), cur)
    bias = _window_bias(qb)
    grid_spec = pltpu.PrefetchScalarGridSpec(
        num_scalar_prefetch=1, grid=(B, n),
        in_specs=[pl.BlockSpec((None, qb, WA_Q), cur),
                  halo(prev), body, halo(nxt), halo(prev), body, halo(nxt),
                  pl.BlockSpec(bias.shape, lambda b, i, s: (0, 0, 0))],
        out_specs=pl.BlockSpec((None, qb, WA_Q), cur))
    return pl.pallas_call(
        _win_attn_kernel, grid_spec=grid_spec,
        out_shape=jax.ShapeDtypeStruct((B, L, WA_Q), BF16),
        compiler_params=_params("parallel", "parallel"), name="window_attention",
    )(sink, q, k, k, k, v, v, v, bias)


def _rope_tables(L):
    inv = ROPE_THETA ** (-jnp.arange(0, WA_HEAD_DIM, 2, dtype=F32) / WA_HEAD_DIM)
    ang = jnp.arange(L, dtype=F32)[:, None] * inv[None, :]
    cos, sin = jnp.cos(ang), jnp.sin(ang)
    reps = LANES // WA_HEAD_DIM
    return jnp.tile(jnp.concatenate([cos, cos], axis=1), (1, reps)), \
        jnp.tile(jnp.concatenate([-sin, sin], axis=1), (1, reps))


def _router_kernel(tm, x_ref, g_ref, wr_ref, hs_ref, mi_ref, mf_ref, cnt_ref, before_ref, carry_ref):
    @pl.when(pl.program_id(0) == 0)
    def _():
        carry_ref[...] = jnp.zeros_like(carry_ref)

    before_ref[...] = jnp.broadcast_to(carry_ref[...], before_ref.shape)
    hf = _rmsnorm(x_ref[...], g_ref[...])
    for s in range(TOK_TILES):
        hs_ref[pl.ds(s, tm, stride=TOK_TILES), :] = hf[:, s * LANES:(s + 1) * LANES]
    logits = _dot_x3(hf, wr_ref[...])
    lane = lax.broadcasted_iota(I32, logits.shape, 1)
    lg = jnp.where(lane < N_EXPERTS, logits, -jnp.inf)
    m1 = jnp.max(lg, axis=1, keepdims=True)
    i1 = jnp.min(jnp.where(lg == m1, lane, LANES), axis=1, keepdims=True)
    lg2 = jnp.where(lane == i1, -jnp.inf, lg)
    m2 = jnp.max(lg2, axis=1, keepdims=True)
    i2 = jnp.min(jnp.where(lg2 == m2, lane, LANES), axis=1, keepdims=True)
    e2 = jnp.exp(m2 - m1)
    g1 = 1.0 / (1.0 + e2)
    g2 = e2 / (1.0 + e2)
    oh1 = lane == i1
    oh2 = lane == i2
    oh = (oh1 | oh2).astype(F32)
    r = lax.broadcasted_iota(I32, (tm, tm), 0)
    c = lax.broadcasted_iota(I32, (tm, tm), 1)
    before = _dot((r > c).astype(BF16), oh.astype(BF16)) + carry_ref[...]
    rank1 = jnp.sum(jnp.where(oh1, before, 0.0), axis=1, keepdims=True).astype(I32)
    rank2 = jnp.sum(jnp.where(oh2, before, 0.0), axis=1, keepdims=True).astype(I32)
    carry_ref[...] += jnp.sum(oh, axis=0, keepdims=True)
    mi_ref[...] = jnp.where(lane == 0, i1, jnp.where(lane == 1, i2, jnp.where(lane == 2, rank1,
                            jnp.where(lane == 3, rank2, 0))))
    mf_ref[...] = jnp.where(lane == 0, g1, jnp.where(lane == 1, g2, 0.0))
    cnt_ref[...] = jnp.broadcast_to(carry_ref[...], cnt_ref.shape)


def moe_router(x, g, w_router, tm):
    T, D = x.shape
    wr = jnp.zeros((D, LANES), F32).at[:, :N_EXPERTS].set(w_router)
    return pl.pallas_call(
        functools.partial(_router_kernel, tm), grid=(T // tm,),
        in_specs=[pl.BlockSpec((tm, D), lambda i: (i, 0)),
                  pl.BlockSpec((1, D), lambda i: (0, 0)),
                  pl.BlockSpec((D, LANES), lambda i: (0, 0))],
        out_specs=[pl.BlockSpec((tm * TOK_TILES, LANES), lambda i: (i, 0)),
                   pl.BlockSpec((tm, LANES), lambda i: (i, 0)),
                   pl.BlockSpec((tm, LANES), lambda i: (i, 0)),
                   pl.BlockSpec((SUBLANES, LANES), lambda i: (0, 0)),
                   pl.BlockSpec((SUBLANES, LANES), lambda i: (i, 0))],
        out_shape=[jax.ShapeDtypeStruct((T * TOK_TILES, LANES), F32),
                   jax.ShapeDtypeStruct((T, LANES), I32),
                   jax.ShapeDtypeStruct((T, LANES), F32),
                   jax.ShapeDtypeStruct((SUBLANES, LANES), F32),
                   jax.ShapeDtypeStruct((T // tm * SUBLANES, LANES), F32)],
        scratch_shapes=[pltpu.VMEM((1, LANES), F32)],
        compiler_params=_params("arbitrary"), name="moe_router",
    )(x, g.reshape(1, D), wr)


def _row_copy(src, src_row, dst, dst_row, sem):
    return pltpu.make_async_copy(src.at[pl.ds(src_row * TOK_TILES, TOK_TILES), :],
                                 dst.at[pl.ds(dst_row * TOK_TILES, TOK_TILES), :], sem)


def _segment_copies(seg_ref, tile, stage_ref, hbm_ref, sem, to_hbm, wait):
    for e in range(N_EXPERTS):
        col = tile * N_EXPERTS + e
        row0, n, off = seg_ref[0, col], seg_ref[1, col], seg_ref[2, col]

        def piece(first, rows):
            a = stage_ref.at[pl.ds((off + first) * TOK_TILES, rows * TOK_TILES), :]
            b = hbm_ref.at[pl.ds((row0 + first) * TOK_TILES, rows * TOK_TILES), :]
            cp = pltpu.make_async_copy(a, b, sem) if to_hbm else pltpu.make_async_copy(b, a, sem)
            cp.wait() if wait else cp.start()

        def full(c, carry):
            piece(c * SEG_CHUNK, SEG_CHUNK)
            return carry

        lax.fori_loop(0, n // SEG_CHUNK, full, 0)
        rem = n % SEG_CHUNK
        for rows in (4, 2, 1):
            covered = (n - rem) + (rem - rem % (2 * rows))
            pl.when((rem & rows) != 0)(functools.partial(piece, covered, rows))


def _dispatch_kernel(tm, seg_ref, tail_ref, pos_ref, hs_ref, xs_ref, stage0_ref, stage1_ref, zero_ref,
                     sem0, sem1, zero_sem):
    i = pl.program_id(0)
    last = pl.num_programs(0) - 1

    @pl.when(i == 0)
    def _():
        zero_ref[...] = jnp.zeros_like(zero_ref)
        for wait in (False, True):
            for e in range(N_EXPERTS):
                def pad_row(r, carry):
                    cp = _row_copy(zero_ref, 0, xs_ref, tail_ref[0, e] + r, zero_sem)
                    cp.wait() if wait else cp.start()
                    return carry

                lax.fori_loop(0, tail_ref[1, e], pad_row, 0)

            def pad_chunk(r, carry):
                row = tail_ref[0, N_EXPERTS] + r * ZERO_ROWS
                cp = pltpu.make_async_copy(
                    zero_ref, xs_ref.at[pl.ds(row * TOK_TILES, ZERO_ROWS * TOK_TILES), :], zero_sem)
                cp.wait() if wait else cp.start()
                return carry

            lax.fori_loop(0, tail_ref[1, N_EXPERTS], pad_chunk, 0)

    def run(stage_ref, sem, other_ref, other_sem):
        @pl.when(i >= 2)
        def _():
            _segment_copies(seg_ref, i - 2, stage_ref, xs_ref, sem, True, True)

        def place(t, carry):
            row = hs_ref[pl.ds(t * TOK_TILES, TOK_TILES), :]
            for k in range(TOP_K):
                stage_ref[pl.ds(pos_ref[k, t] * TOK_TILES, TOK_TILES), :] = row
            return carry

        lax.fori_loop(0, tm, place, 0, unroll=8)
        _segment_copies(seg_ref, i, stage_ref, xs_ref, sem, True, False)

        @pl.when(i == last)
        def _():
            @pl.when(i >= 1)
            def _():
                _segment_copies(seg_ref, i - 1, other_ref, xs_ref, other_sem, True, True)

            _segment_copies(seg_ref, i, stage_ref, xs_ref, sem, True, True)

    pl.when(i % 2 == 0)(functools.partial(run, stage0_ref, sem0, stage1_ref, sem1))
    pl.when(i % 2 == 1)(functools.partial(run, stage1_ref, sem1, stage0_ref, sem0))


def moe_dispatch(hs, pos, seg, tail, n_rows, tm):
    T = pos.shape[1]
    stage = pltpu.VMEM((TOP_K * tm * TOK_TILES, LANES), F32)
    grid_spec = pltpu.PrefetchScalarGridSpec(
        num_scalar_prefetch=2, grid=(T // tm,),
        in_specs=[pl.BlockSpec((TOP_K, tm), lambda i, sg, tl: (0, i), memory_space=pltpu.SMEM),
                  pl.BlockSpec((tm * TOK_TILES, LANES), lambda i, sg, tl: (i, 0))],
        out_specs=pl.BlockSpec(memory_space=pl.ANY),
        scratch_shapes=[stage, stage, pltpu.VMEM((ZERO_ROWS * TOK_TILES, LANES), F32),
                        pltpu.SemaphoreType.DMA(()), pltpu.SemaphoreType.DMA(()),
                        pltpu.SemaphoreType.DMA(())])
    return pl.pallas_call(
        functools.partial(_dispatch_kernel, tm), grid_spec=grid_spec,
        out_shape=jax.ShapeDtypeStruct((n_rows * TOK_TILES, LANES), F32),
        compiler_params=_params("arbitrary"), name="moe_dispatch",
    )(seg, tail, pos, hs)


def _expert_kernel(tm, be_ref, nu_ref, xs_ref, wg_ref, wu_ref, wd_ref, ys_ref, h_ref, acc_ref):
    b = pl.program_id(0)
    j = pl.program_id(1)
    last = pl.num_programs(1) - 1
    used = b < nu_ref[0]

    @pl.when(used & (j == 0))
    def _():
        for s in range(TOK_TILES):
            h_ref[:, s * LANES:(s + 1) * LANES] = xs_ref[pl.ds(s, tm, stride=TOK_TILES), :].astype(BF16)
        acc_ref[...] = jnp.zeros_like(acc_ref)

    @pl.when(used)
    def _():
        _swiglu_step(h_ref[...], wg_ref, wu_ref, wd_ref, acc_ref)

    @pl.when(used & (j == last))
    def _():
        for s in range(TOK_TILES):
            ys_ref[pl.ds(s, tm, stride=TOK_TILES), :] = acc_ref[:, s * LANES:(s + 1) * LANES]

    @pl.when(jnp.logical_not(used) & (j == last))
    def _():
        ys_ref[...] = jnp.zeros_like(ys_ref)


def moe_experts(xs, blk_expert, n_used, w_gate, w_up, w_down, tm):
    E, D, F = w_gate.shape
    nb = xs.shape[0] // (tm * TOK_TILES)
    tf = _tile(F, FF_TILE)
    nj = F // tf
    jj = lambda b, j, be, nu: jnp.where(b < nu[0], j, nj - 1)
    grid_spec = pltpu.PrefetchScalarGridSpec(
        num_scalar_prefetch=2, grid=(nb, nj),
        in_specs=[pl.BlockSpec((tm * TOK_TILES, LANES), lambda b, j, be, nu: (jnp.minimum(b, nu[0] - 1), 0)),
                  pl.BlockSpec((None, D, tf), lambda b, j, be, nu: (be[b], 0, jj(b, j, be, nu))),
                  pl.BlockSpec((None, D, tf), lambda b, j, be, nu: (be[b], 0, jj(b, j, be, nu))),
                  pl.BlockSpec((None, tf, D), lambda b, j, be, nu: (be[b], jj(b, j, be, nu), 0))],
        out_specs=pl.BlockSpec((tm * TOK_TILES, LANES), lambda b, j, be, nu: (b, 0)),
        scratch_shapes=[pltpu.VMEM((tm, D), BF16), pltpu.VMEM((tm, D), F32)])
    return pl.pallas_call(
        functools.partial(_expert_kernel, tm), grid_spec=grid_spec,
        out_shape=jax.ShapeDtypeStruct(xs.shape, F32),
        compiler_params=_params("arbitrary", "arbitrary"), name="moe_experts",
    )(blk_expert, n_used, xs, w_gate, w_up, w_down)


def _combine_kernel(tm, seg_ref, pos_ref, gate_ref, ys_ref, x_ref, g_ref, o_ref,
                    stage0_ref, stage1_ref, ybuf_ref, y_ref, sem0, sem1):
    i = pl.program_id(0)
    n = pl.num_programs(0)

    def run(stage_ref, sem, other_ref, other_sem):
        @pl.when(i == 0)
        def _():
            _segment_copies(seg_ref, i, stage_ref, ys_ref, sem, False, False)

        @pl.when(i + 1 < n)
        def _():
            _segment_copies(seg_ref, i + 1, other_ref, ys_ref, other_sem, False, False)

        _segment_copies(seg_ref, i, stage_ref, ys_ref, sem, False, True)

        def pick(t, carry):
            a = stage_ref[pl.ds(pos_ref[0, t] * TOK_TILES, TOK_TILES), :]
            b = stage_ref[pl.ds(pos_ref[1, t] * TOK_TILES, TOK_TILES), :]
            ybuf_ref[pl.ds(t * TOK_TILES, TOK_TILES), :] = a * gate_ref[0, t] + b * gate_ref[1, t]
            return carry

        lax.fori_loop(0, tm, pick, 0, unroll=8)

    pl.when(i % 2 == 0)(functools.partial(run, stage0_ref, sem0, stage1_ref, sem1))
    pl.when(i % 2 == 1)(functools.partial(run, stage1_ref, sem1, stage0_ref, sem0))
    for s in range(TOK_TILES):
        y_ref[:, s * LANES:(s + 1) * LANES] = ybuf_ref[pl.ds(s, tm, stride=TOK_TILES), :]
    o_ref[...] = _rmsnorm(x_ref[...] + y_ref[...], g_ref[...])


def moe_combine(ys, pos, gates, seg, x, g_final, tm):
    T, D = x.shape
    stage = pltpu.VMEM((TOP_K * tm * TOK_TILES, LANES), F32)
    grid_spec = pltpu.PrefetchScalarGridSpec(
        num_scalar_prefetch=1, grid=(T // tm,),
        in_specs=[pl.BlockSpec((TOP_K, tm), lambda i, sg: (0, i), memory_space=pltpu.SMEM),
                  pl.BlockSpec((TOP_K, tm), lambda i, sg: (0, i), memory_space=pltpu.SMEM),
                  pl.BlockSpec(memory_space=pl.ANY),
                  pl.BlockSpec((tm, D), lambda i, sg: (i, 0)),
                  pl.BlockSpec((1, D), lambda i, sg: (0, 0))],
        out_specs=pl.BlockSpec((tm, D), lambda i, sg: (i, 0)),
        scratch_shapes=[stage, stage, pltpu.VMEM((tm * TOK_TILES, LANES), F32), pltpu.VMEM((tm, D), F32),
                        pltpu.SemaphoreType.DMA(()), pltpu.SemaphoreType.DMA(())])
    return pl.pallas_call(
        functools.partial(_combine_kernel, tm), grid_spec=grid_spec,
        out_shape=jax.ShapeDtypeStruct((T, D), F32),
        compiler_params=_params("arbitrary"), name="moe_combine",
    )(seg, pos, gates, ys, x, g_final.reshape(1, D))


def moe_layer(x, g_ffn, g_final, w_router, w_gate, w_up, w_down):
    T = x.shape[0]
    tm = MOE_ROWS_LARGE if T * TOP_K >= MOE_ROWS_LARGE * N_EXPERTS * 8 else MOE_ROWS_SMALL
    tt = _tile(T, MOE_TILE)
    hs, meta_i, meta_f, cnt, before = moe_router(x, g_ffn, w_router, tt)
    counts = cnt[0, :N_EXPERTS].astype(I32)
    padded = (counts + tm - 1) // tm * tm
    pad_end = jnp.cumsum(padded)
    pad_start = pad_end - padded
    idx = meta_i[:, 0:TOP_K].T
    rank = meta_i[:, TOP_K:2 * TOP_K].T
    before = before[::SUBLANES, :N_EXPERTS].astype(I32)
    seg_len = jnp.concatenate([before[1:], counts[None]], axis=0) - before
    stage_off = jnp.cumsum(seg_len, axis=1) - seg_len
    seg = jnp.stack([(pad_start[None] + before).reshape(-1), seg_len.reshape(-1),
                     stage_off.reshape(-1)]).astype(I32)
    in_tile = jnp.repeat(stage_off - before, tt, axis=0)
    chosen = idx[..., None] == jnp.arange(N_EXPERTS, dtype=I32)
    pos = (rank + jnp.sum(jnp.where(chosen, in_tile[None], 0), axis=-1)).astype(I32)
    gates = meta_f[:, 0:TOP_K].T
    nb = (T * TOP_K) // tm + N_EXPERTS
    n_used = (pad_end[-1] // tm).astype(I32)
    blk_row = jnp.minimum(jnp.arange(nb, dtype=I32), n_used - 1) * tm
    blk_expert = jnp.minimum(jnp.searchsorted(pad_end, blk_row, side='right'), N_EXPERTS - 1).astype(I32)
    tail = jnp.stack([jnp.append(pad_start + counts, pad_end[-1]),
                      jnp.append(padded - counts, (nb * tm - pad_end[-1]) // ZERO_ROWS)]).astype(I32)
    xs = moe_dispatch(hs, pos, seg, tail, nb * tm, tt)
    ys = moe_experts(xs, blk_expert, n_used.reshape(1), w_gate, w_up, w_down, tm)
    return moe_combine(ys, pos, gates, seg, x, g_final, tt)


def _prep_weights(g_mix, g_mem, g_ffn, g_final, a_w_in, a_conv, a_log, a_dt_bias, a_o_norm,
                  a_mem_kv, a_w_out, b_w_in, b_sink, b_mem_kv, b_w_out, f_w_gate, f_w_up, f_w_down,
                  e_router, e_w_gate, e_w_up, e_w_down):
    s1 = DN_QKV
    s2 = s1 + DN_V
    s3 = s2 + 4 * DN_HEADS
    w = a_w_in[0]
    gates_w = jnp.zeros((D_MODEL, LANES), F32).at[:, :4 * DN_HEADS].set(w[:, s2:s3])
    wb = b_w_in[0]
    return dict(
        g_mix=g_mix, g_mem=g_mem, g_ffn=g_ffn, g_final=g_final,
        a_in=[w[:, :s1].astype(BF16), w[:, s1:s2].astype(BF16), w[:, s3:].astype(BF16), gates_w],
        a_conv=a_conv[0], a_log=a_log[0], a_dt_bias=a_dt_bias[0], a_o_norm=a_o_norm[0],
        a_mem_kv=a_mem_kv[0].astype(BF16), a_w_out=a_w_out[0].astype(BF16),
        b_in=[wb[:, :WA_Q].astype(BF16), wb[:, WA_Q:WA_Q + WA_KV].astype(BF16),
              wb[:, WA_Q + WA_KV:WA_Q + 2 * WA_KV].astype(BF16), wb[:, WA_Q + 2 * WA_KV:].astype(BF16)],
        b_sink=b_sink[0], b_mem_kv=b_mem_kv[0].astype(BF16), b_w_out=b_w_out[0].astype(BF16),
        f_w_gate=f_w_gate[0].astype(BF16), f_w_up=f_w_up[0].astype(BF16), f_w_down=f_w_down[0].astype(BF16),
        e_router=e_router[0], e_w_gate=e_w_gate[0].astype(BF16), e_w_up=e_w_up[0].astype(BF16),
        e_w_down=e_w_down[0].astype(BF16))


def _trunk(x, mem, p):
    B, L, D = x.shape
    M = mem.shape[1]
    T = B * L
    x2 = x.reshape(T, D)
    mem2 = mem.reshape(B * M, D)

    qkv, z, qc, gates = norm_proj(x2, p['g_mix'][0], p['a_in'], [BF16, BF16, BF16, F32],
                                  scales=(None, None, MX_HEAD_DIM ** -0.5 * LOG2_E, None))
    (kv,) = norm_proj(mem2, p['g_mem'][0], [p['a_mem_kv']], [BF16])
    q, k, v, pack, packt = dn_prep(qkv.reshape(B, L, DN_QKV), gates.reshape(B, L, LANES),
                                   p['a_conv'], p['a_log'], p['a_dt_bias'])
    o_f, o_b = delta_rule(q, k, v, pack, packt)
    xo = mem_attention(qc.reshape(B, L, MX_W), kv.reshape(B, M, 2 * MX_W))
    x1 = out_proj_a(o_f, o_b, z.reshape(B, L, DN_V), xo, x, p['a_o_norm'], p['a_w_out'])
    x1 = ffn(x1.reshape(T, D), p['g_ffn'][0], p['f_w_gate'], p['f_w_up'], p['f_w_down'])

    cos, sin_signed = _rope_tables(L)
    q, k, v, qc = norm_proj(x1, p['g_mix'][1], p['b_in'], [BF16, BF16, BF16, BF16],
                            scales=(WA_HEAD_DIM ** -0.5 * LOG2_E, None, None, MX_HEAD_DIM ** -0.5 * LOG2_E),
                            rope=(cos, sin_signed, (True, True, False, False)))
    (kv,) = norm_proj(mem2, p['g_mem'][1], [p['b_mem_kv']], [BF16])
    o = window_attention(q.reshape(B, L, WA_Q), k.reshape(B, L, WA_KV), v.reshape(B, L, WA_KV),
                         p['b_sink'])
    xo = mem_attention(qc.reshape(B, L, MX_W), kv.reshape(B, M, 2 * MX_W))
    x2 = out_proj_b(o, xo, x1.reshape(B, L, D), p['b_w_out'])
    y = moe_layer(x2.reshape(T, D), p['g_ffn'][1], p['g_final'], p['e_router'],
                  p['e_w_gate'], p['e_w_up'], p['e_w_down'])
    return y.reshape(B, L, D)


def kernel(x_prompt, x_sample, mem_prompt, mem_sample, g_mix, g_mem, g_ffn, g_final, a_w_in, a_conv,
           a_log, a_dt_bias, a_o_norm, a_mem_kv, a_w_out, b_w_in, b_sink, b_mem_kv, b_w_out,
           f_w_gate, f_w_up, f_w_down, e_router, e_w_gate, e_w_up, e_w_down):
    p = _prep_weights(g_mix, g_mem, g_ffn, g_final, a_w_in, a_conv, a_log, a_dt_bias, a_o_norm,
                      a_mem_kv, a_w_out, b_w_in, b_sink, b_mem_kv, b_w_out, f_w_gate, f_w_up, f_w_down,
                      e_router, e_w_gate, e_w_up, e_w_down)
    return (_trunk(x_prompt, mem_prompt, p), _trunk(x_sample, mem_sample, p))
```

```python
import functools
import math

import jax
import jax.numpy as jnp
from jax import lax
from jax.experimental import pallas as pl
from jax.experimental.pallas import tpu as pltpu

F32 = jnp.float32
BF16 = jnp.bfloat16
I32 = jnp.int32

EPS = 1e-6
LOG2_E = math.log2(math.e)
D_MODEL = 1024
LANES = 128
SUBLANES = 8
TOK_TILES = D_MODEL // LANES
VMEM_LIMIT_BYTES = 56 * 1024 * 1024

DN_HEADS = 6
DN_D = 128
DN_CONV = 5
DN_CHUNK = 64
DN_QKV = 3 * DN_HEADS * DN_D
DN_V = DN_HEADS * DN_D
DN_TILE = 256
CONV_HALO = 16
GATE_ROWS = 48

WA_Q_HEADS = 12
WA_KV_HEADS = 4
WA_GROUP = WA_Q_HEADS // WA_KV_HEADS
WA_HEAD_DIM = 64
WA_BLOCK = 128
WA_QBLOCK = 256
WA_Q = WA_Q_HEADS * WA_HEAD_DIM
WA_KV = WA_KV_HEADS * WA_HEAD_DIM
ROPE_THETA = 10000.0

MX_HEADS = 4
MX_HEAD_DIM = 64
MX_W = MX_HEADS * MX_HEAD_DIM

D_FF = 3584
N_EXPERTS = 8
TOP_K = 2
FF_TILE = 512
MOE_TILE = 512
MOE_ROWS_SMALL = 512
MOE_ROWS_LARGE = 1024
ZERO_ROWS = 64
SEG_CHUNK = 8


def _params(*sem):
    return pltpu.CompilerParams(dimension_semantics=sem, vmem_limit_bytes=VMEM_LIMIT_BYTES)


def _tile(n, pref):
    t = min(n, pref)
    assert n % t == 0, (n, pref)
    return t


def _dot(a, b):
    return jnp.dot(a, b, preferred_element_type=F32)


def _dot_nt(a, b):
    return lax.dot_general(a, b, (((1,), (1,)), ((), ())), preferred_element_type=F32)


def _dot_tn(a, b):
    return lax.dot_general(a, b, (((0,), (0,)), ((), ())), preferred_element_type=F32)


def _split3(x):
    hi = x.astype(BF16)
    r1 = x - hi.astype(F32)
    mid = r1.astype(BF16)
    lo = (r1 - mid.astype(F32)).astype(BF16)
    return hi, mid, lo


def _dot_nt_x3(w, h):
    w_hi = w.astype(BF16)
    w_lo = (w - w_hi.astype(F32)).astype(BF16)
    h_hi = h.astype(BF16)
    h_lo = (h - h_hi.astype(F32)).astype(BF16)
    return _dot_nt(w_hi, h_hi) + _dot_nt(w_lo, h_hi) + _dot_nt(w_hi, h_lo)


def _rmsnorm(x, g):
    ms = jnp.mean(x * x, axis=-1, keepdims=True)
    return x * lax.rsqrt(ms + EPS) * g


def _silu(x):
    return x * jax.nn.sigmoid(x)


def _rope(x, cos, sin_signed):
    n, w = x.shape
    lane = lax.broadcasted_iota(I32, (n, LANES), 1)
    first_half = (lane % WA_HEAD_DIM) < (WA_HEAD_DIM // 2)
    outs = []
    for gidx in range(w // LANES):
        xs = x[:, gidx * LANES:(gidx + 1) * LANES]
        rot = jnp.where(first_half, pltpu.roll(xs, LANES - WA_HEAD_DIM // 2, 1),
                        pltpu.roll(xs, WA_HEAD_DIM // 2, 1))
        outs.append(xs * cos + rot * sin_signed)
    return jnp.concatenate(outs, axis=1)


def _norm_proj_kernel(n_out, scales, rope_flags, x_ref, g_ref, *refs):
    if rope_flags is not None:
        cos_ref, sin_ref, refs = refs[0], refs[1], refs[2:]
    w_refs, o_refs = refs[:n_out], refs[n_out:]
    hf = _rmsnorm(x_ref[...], g_ref[...])
    hb = hf.astype(BF16)
    for idx, (w_ref, o_ref) in enumerate(zip(w_refs, o_refs)):
        if w_ref.dtype == F32:
            y = _dot_nt_x3(w_ref[...], hf)
        else:
            y = _dot(hb, w_ref[...])
        if rope_flags is not None and rope_flags[idx]:
            y = _rope(y, cos_ref[...], sin_ref[...])
        if scales is not None and scales[idx] is not None:
            y = y * scales[idx]
        o_ref[...] = y.astype(o_ref.dtype)


def norm_proj(x, g, weights, out_dtypes, scales=None, rope=None, tm_pref=512):
    T, D = x.shape
    tm = _tile(T, tm_pref)
    n_out = len(weights)
    args = [x, g.reshape(1, D)]
    extra_specs = []
    flags = None
    if rope is not None:
        cos, sin_signed, flags = rope
        tm = _tile(cos.shape[0], tm)
        per_seq = cos.shape[0] // tm
        extra_specs = [pl.BlockSpec((tm, LANES), lambda i: (i % per_seq, 0))] * 2
        args += [cos, sin_signed]
    in_specs = [pl.BlockSpec((tm, D), lambda i: (i, 0)), pl.BlockSpec((1, D), lambda i: (0, 0))]
    in_specs += extra_specs + [pl.BlockSpec(w.shape, lambda i: (0, 0)) for w in weights]
    out_specs = [pl.BlockSpec((w.shape[0], tm), lambda i: (0, i)) if w.dtype == F32 else
                 pl.BlockSpec((tm, w.shape[1]), lambda i: (i, 0)) for w in weights]
    out_shape = [jax.ShapeDtypeStruct((w.shape[0], T) if w.dtype == F32 else (T, w.shape[1]), dt)
                 for w, dt in zip(weights, out_dtypes)]
    return pl.pallas_call(
        functools.partial(_norm_proj_kernel, n_out, scales, flags),
        grid=(T // tm,), in_specs=in_specs, out_specs=out_specs, out_shape=out_shape,
        compiler_params=_params("parallel"), name="norm_proj",
    )(*args, *weights)


def _dn_prep_kernel(tl, cur_ref, prev_ref, next_ref, gates_ref, cw_ref, alog_ref, dtb_ref, shift_ref,
                    q_ref, k_ref, v_ref, pack_ref, packt_ref):
    i = pl.program_id(1)
    nt = pl.num_programs(1)
    pad = (DN_CONV - 1) // 2
    taps = [j for j in range(DN_CONV) if j != pad]
    p8 = prev_ref[CONV_HALO - SUBLANES:, :].astype(F32) * (i > 0).astype(F32)
    n8 = next_ref[0:SUBLANES, :].astype(F32) * (i < nt - 1).astype(F32)
    row8 = lax.broadcasted_iota(I32, (SUBLANES, 1), 0)
    first = None
    last = None
    for j in range(DN_CONV):
        off = j - pad
        if off < 0:
            t = jnp.where(row8 < -off, pltpu.roll(p8, -off, 0), 0.0) * cw_ref[j:j + 1, :]
            first = t if first is None else first + t
        elif off > 0:
            t = jnp.where(row8 >= SUBLANES - off, pltpu.roll(n8, SUBLANES - off, 0), 0.0) * cw_ref[j:j + 1, :]
            last = t if last is None else last + t
    mxu_cols = 2 * LANES
    moved = None
    for cg in range(DN_QKV // LANES):
        sl = slice(cg * LANES, (cg + 1) * LANES)
        if cg % 2 == 0:
            x16 = cur_ref[:, cg * LANES:cg * LANES + mxu_cols]
            moved = {j: _dot(shift_ref[n], x16) for n, j in enumerate(taps)}
            moved[pad] = x16.astype(F32)
        half = slice((cg % 2) * LANES, (cg % 2 + 1) * LANES)
        acc = None
        for j in range(DN_CONV):
            term = moved[j][:, half] * cw_ref[j:j + 1, sl]
            acc = term if acc is None else acc + term
        acc = acc + jnp.concatenate(
            [first[:, sl], jnp.zeros((tl - 2 * SUBLANES, LANES), F32), last[:, sl]], axis=0)
        y = _silu(acc)
        if cg < DN_HEADS:
            y = y * lax.rsqrt(jnp.sum(y * y, axis=-1, keepdims=True) + EPS) * (DN_D ** -0.5)
            q_ref[cg] = y.astype(q_ref.dtype)
        elif cg < 2 * DN_HEADS:
            y = y * lax.rsqrt(jnp.sum(y * y, axis=-1, keepdims=True) + EPS)
            k_ref[cg - DN_HEADS] = y.astype(k_ref.dtype)
        else:
            v_ref[cg - 2 * DN_HEADS] = y.astype(v_ref.dtype)

    gt = gates_ref[...]
    row = lax.broadcasted_iota(I32, gt.shape, 0)
    beta = jax.nn.sigmoid(gt)
    xg = gt + dtb_ref[...]
    softplus = jnp.maximum(xg, 0.0) + jnp.log1p(jnp.exp(-jnp.abs(xg)))
    g = -jnp.exp(alog_ref[...]) * softplus
    g = jnp.where((row >= 2 * DN_HEADS) & (row < 4 * DN_HEADS), g, 0.0)
    r = lax.broadcasted_iota(I32, (tl, tl), 0)
    c = lax.broadcasted_iota(I32, (tl, tl), 1)
    same = (r // DN_CHUNK) == (c // DN_CHUNK)
    m_fwd = (same & (r <= c)).astype(BF16)
    m_bwd = (same & (r >= c)).astype(BF16)
    m_all = same.astype(BF16)
    hi, mid, lo = _split3(g)
    g_fwd = _dot(hi, m_fwd) + _dot(mid, m_fwd) + _dot(lo, m_fwd)
    g_bwd = _dot(hi, m_bwd) + _dot(mid, m_bwd) + _dot(lo, m_bwd)
    g_tot = pltpu.roll(_dot(hi, m_all) + _dot(mid, m_all) + _dot(lo, m_all), 2 * DN_HEADS, 0)
    packt = jnp.where(row < 2 * DN_HEADS, beta,
                      jnp.where(row < 3 * DN_HEADS, g_fwd,
                                jnp.where(row < 4 * DN_HEADS, g_bwd,
                                          jnp.where(row < 6 * DN_HEADS, g_tot, 0.0))))
    packt = jnp.concatenate([packt, jnp.zeros((LANES - GATE_ROWS, tl), F32)], axis=0)
    packt_ref[...] = packt
    pack_ref[...] = packt.T


def dn_prep(qkv, gates, conv_w, a_log, dt_bias):
    B, L, _ = qkv.shape
    tl = _tile(L, DN_TILE)
    nt = L // tl
    hb = tl // CONV_HALO
    nhalo = L // CONV_HALO
    cw = jnp.zeros((SUBLANES, DN_QKV), F32).at[:DN_CONV].set(conv_w)
    alog = jnp.zeros((GATE_ROWS, 1), F32).at[2 * DN_HEADS:4 * DN_HEADS, 0].set(a_log.reshape(-1))
    dtb = jnp.zeros((GATE_ROWS, 1), F32).at[2 * DN_HEADS:4 * DN_HEADS, 0].set(dt_bias.reshape(-1))
    pad = (DN_CONV - 1) // 2
    t_idx = jnp.arange(tl)
    shift = jnp.stack([(t_idx[None, :] == t_idx[:, None] + (j - pad)).astype(BF16)
                       for j in range(DN_CONV) if j != pad])
    hm = jax.ShapeDtypeStruct((B, DN_HEADS, L, DN_D), BF16)
    hm_spec = pl.BlockSpec((None, DN_HEADS, tl, DN_D), lambda b, i: (b, 0, i, 0))
    return pl.pallas_call(
        functools.partial(_dn_prep_kernel, tl),
        grid=(B, nt),
        in_specs=[
            pl.BlockSpec((None, tl, DN_QKV), lambda b, i: (b, i, 0)),
            pl.BlockSpec((None, CONV_HALO, DN_QKV), lambda b, i: (b, jnp.maximum(i * hb - 1, 0), 0)),
            pl.BlockSpec((None, CONV_HALO, DN_QKV), lambda b, i: (b, jnp.minimum((i + 1) * hb, nhalo - 1), 0)),
            pl.BlockSpec((GATE_ROWS, tl), lambda b, i: (0, b * nt + i)),
            pl.BlockSpec((SUBLANES, DN_QKV), lambda b, i: (0, 0)),
            pl.BlockSpec((GATE_ROWS, 1), lambda b, i: (0, 0)),
            pl.BlockSpec((GATE_ROWS, 1), lambda b, i: (0, 0)),
            pl.BlockSpec(shift.shape, lambda b, i: (0, 0, 0)),
        ],
        out_specs=[hm_spec, hm_spec, hm_spec,
                   pl.BlockSpec((None, tl, LANES), lambda b, i: (b, i, 0)),
                   pl.BlockSpec((None, LANES, tl), lambda b, i: (b, 0, i))],
        out_shape=[hm, hm, hm, jax.ShapeDtypeStruct((B, L, LANES), F32),
                   jax.ShapeDtypeStruct((B, LANES, L), F32)],
        compiler_params=_params("parallel", "parallel"), name="dn_prep",
    )(qkv, qkv, qkv, gates, cw, alog, dtb, shift)


def _dn_tile_stage(units, tl, refs_f, refs_b, masks_f, masks_b, u_ref, wq_ref, kd_ref, aqk_ref):
    n = len(units)
    nchunk = tl // DN_CHUNK
    refs = [refs_f if d == 0 else refs_b for d, _ in units]
    masks = [masks_f if d == 0 else masks_b for d, _ in units]
    col = []
    for (d, h), (q_ref, k_ref, v_ref, pack_ref, packt_ref) in zip(units, refs):
        pick = lambda li: pack_ref[:, li:li + 1]
        li_g = (2 + d) * DN_HEADS + h
        col.append((pick(d * DN_HEADS + h), pick(li_g), pick((4 + d) * DN_HEADS + h),
                    packt_ref[li_g:li_g + 1, :]))
    q16 = [r[0][h] for (d, h), r in zip(units, refs)]
    k16 = [r[1][h] for (d, h), r in zip(units, refs)]
    kf = [x.astype(F32) for x in k16]
    kb = [kf[i] * col[i][0] for i in range(n)]
    decay = [jnp.exp(jnp.where(masks[i][0], col[i][1] - col[i][3], -jnp.inf)) for i in range(n)]
    kk = [_dot_nt(kb[i].astype(BF16), k16[i]) for i in range(n)]
    qk = [_dot_nt(q16[i], k16[i]) for i in range(n)]
    a = [jnp.where(masks[i][1], kk[i] * decay[i], 0.0) for i in range(n)]
    for i, (d, h) in enumerate(units):
        a_qk = (qk[i] * decay[i]).astype(BF16)
        for c in range(nchunk):
            rs = slice(c * DN_CHUNK, (c + 1) * DN_CHUNK)
            aqk_ref[d * DN_HEADS + h, rs, :] = a_qk[rs, rs]
    same, eye_wide = masks[0][2], masks[0][3]

    def wide(x):
        out = x[0:DN_CHUNK]
        for c in range(1, nchunk):
            out = out + x[c * DN_CHUNK:(c + 1) * DN_CHUNK]
        return out

    def block_diag(x16):
        return jnp.where(same, jnp.concatenate([x16] * nchunk, axis=0), jnp.zeros((), BF16))

    t_w = [eye_wide - wide(x) for x in a]
    p_w16 = [wide(x).astype(BF16) for x in a]
    p_bd16 = [x.astype(BF16) for x in a]
    for it in range(int(math.log2(DN_CHUNK)) - 1):
        p_w16 = [_dot(p_w16[i], p_bd16[i]).astype(BF16) for i in range(n)]
        p_bd16 = [block_diag(x) for x in p_w16]
        t_w = [t_w[i] + _dot(t_w[i].astype(BF16), p_bd16[i]) for i in range(n)]
    t_bd16 = [block_diag(x.astype(BF16)) for x in t_w]
    e_g = [jnp.exp(col[i][1]) for i in range(n)]
    rhs = [jnp.concatenate([refs[i][2][units[i][1]].astype(F32) * col[i][0], kb[i] * e_g[i]],
                           axis=1).astype(BF16) for i in range(n)]
    uw = [_dot(t_bd16[i], rhs[i]) for i in range(n)]
    for i, (d, h) in enumerate(units):
        ui = d * DN_HEADS + h
        u_ref[ui] = uw[i][:, :DN_D]
        w16 = uw[i][:, DN_D:].astype(BF16)
        qd16 = (q16[i].astype(F32) * e_g[i]).astype(BF16)
        for c in range(nchunk):
            rs = slice(c * DN_CHUNK, (c + 1) * DN_CHUNK)
            wq_ref[ui, 2 * c * DN_CHUNK:(2 * c + 1) * DN_CHUNK, :] = w16[rs]
            wq_ref[ui, (2 * c + 1) * DN_CHUNK:(2 * c + 2) * DN_CHUNK, :] = qd16[rs]
        kd_ref[ui] = (kf[i] * jnp.exp(col[i][2] - col[i][1])).astype(BF16)


def _dn_scan_stage(tl, ptf_ref, ptb_ref, of_ref, ob_ref, s_ref, u_ref, wq_ref, kd_ref, aqk_ref):
    nchunk = tl // DN_CHUNK
    chains = [(d, h) for h in range(DN_HEADS) for d in range(2)]
    ids = [d * DN_HEADS + h for d, h in chains]
    for step in range(nchunk):
        rows = []
        for d, h in chains:
            c = step if d == 0 else nchunk - 1 - step
            rows.append(slice(c * DN_CHUNK, (c + 1) * DN_CHUNK))
        s = [s_ref[ui] for ui in ids]
        sb = [x.astype(BF16) for x in s]
        ws = [_dot(wq_ref[ui, 2 * rs.start:2 * rs.stop, :], b) for ui, rs, b in zip(ids, rows, sb)]
        v_new = [(u_ref[ui, rs, :] - x[:DN_CHUNK]).astype(BF16) for ui, rs, x in zip(ids, rows, ws)]
        o = [x[DN_CHUNK:] + _dot(aqk_ref[ui, rs, :], vn) for ui, rs, x, vn in zip(ids, rows, ws, v_new)]
        for (d, h), ui, rs, x, vn, oo in zip(chains, ids, rows, s, v_new, o):
            pt_ref = ptf_ref if d == 0 else ptb_ref
            li_tot = (4 + d) * DN_HEADS + h
            tot = pt_ref[li_tot:li_tot + 1, rs.start:rs.start + 1]
            s_ref[ui] = x * jnp.exp(tot) + _dot_tn(kd_ref[ui, rs, :], vn)
            (of_ref if d == 0 else ob_ref)[h, rs, :] = oo.astype(of_ref.dtype)


def _dn_kernel(tl, qf_ref, kf_ref, vf_ref, pf_ref, ptf_ref, qb_ref, kb_ref, vb_ref, pb_ref, ptb_ref,
               of_ref, ob_ref, s_ref, u_ref, wq_ref, kd_ref, aqk_ref):
    @pl.when(pl.program_id(1) == 0)
    def _():
        s_ref[...] = jnp.zeros_like(s_ref)

    r = lax.broadcasted_iota(I32, (tl, tl), 0)
    c = lax.broadcasted_iota(I32, (tl, tl), 1)
    same = (r // DN_CHUNK) == (c // DN_CHUNK)
    eye_wide = (lax.broadcasted_iota(I32, (DN_CHUNK, tl), 0) ==
                lax.broadcasted_iota(I32, (DN_CHUNK, tl), 1) % DN_CHUNK).astype(F32)
    masks_f = (same & (r >= c), same & (r > c), same, eye_wide)
    masks_b = (same & (r <= c), same & (r < c), same, eye_wide)
    refs_f = (qf_ref, kf_ref, vf_ref, pf_ref, ptf_ref)
    refs_b = (qb_ref, kb_ref, vb_ref, pb_ref, ptb_ref)

    units = [(d, h) for h in range(DN_HEADS) for d in range(2)]
    _dn_tile_stage(units, tl, refs_f, refs_b, masks_f, masks_b, u_ref, wq_ref, kd_ref, aqk_ref)
    _dn_scan_stage(tl, ptf_ref, ptb_ref, of_ref, ob_ref, s_ref, u_ref, wq_ref, kd_ref, aqk_ref)


def delta_rule(q, k, v, pack, packt):
    B, H, L, _ = q.shape
    tl = _tile(L, DN_TILE)
    nt = L // tl
    fwd = lambda b, i: (b, 0, i, 0)
    bwd = lambda b, i: (b, 0, nt - 1 - i, 0)
    hm_f = pl.BlockSpec((None, H, tl, DN_D), fwd)
    hm_b = pl.BlockSpec((None, H, tl, DN_D), bwd)
    pk_f = pl.BlockSpec((None, tl, LANES), lambda b, i: (b, i, 0))
    pk_b = pl.BlockSpec((None, tl, LANES), lambda b, i: (b, nt - 1 - i, 0))
    pt_f = pl.BlockSpec((None, LANES, tl), lambda b, i: (b, 0, i))
    pt_b = pl.BlockSpec((None, LANES, tl), lambda b, i: (b, 0, nt - 1 - i))
    o = jax.ShapeDtypeStruct((B, H, L, DN_D), BF16)
    return pl.pallas_call(
        functools.partial(_dn_kernel, tl),
        grid=(B, nt),
        in_specs=[hm_f, hm_f, hm_f, pk_f, pt_f, hm_b, hm_b, hm_b, pk_b, pt_b],
        out_specs=[hm_f, hm_b], out_shape=[o, o],
        scratch_shapes=[pltpu.VMEM((2 * H, DN_D, DN_D), F32),
                        pltpu.VMEM((2 * H, tl, DN_D), F32),
                        pltpu.VMEM((2 * H, 2 * tl, DN_D), BF16),
                        pltpu.VMEM((2 * H, tl, DN_D), BF16),
                        pltpu.VMEM((2 * H, tl, DN_CHUNK), BF16)],
        compiler_params=_params("parallel", "arbitrary"), name="delta_rule",
    )(q, k, v, pack, packt, q, k, v, pack, packt)


def _mem_attn_kernel(q_ref, kv_ref, o_ref):
    q = q_ref[...]
    kv = kv_ref[...]
    heads = range(MX_HEADS)
    ks = [slice(h * MX_HEAD_DIM, (h + 1) * MX_HEAD_DIM) for h in heads]
    vs = [slice(MX_W + h * MX_HEAD_DIM, MX_W + (h + 1) * MX_HEAD_DIM) for h in heads]
    s = [_dot_nt(kv[:, ks[h]], q[:, ks[h]]) for h in heads]
    m = [jnp.max(s[h], axis=0, keepdims=True) for h in heads]
    p = [jnp.exp2(s[h] - m[h]) for h in heads]
    den = [jnp.sum(p[h], axis=0, keepdims=True) for h in heads]
    o_t = [_dot_tn(kv[:, vs[h]], p[h].astype(BF16)) / den[h] for h in heads]
    o_ref[...] = jnp.concatenate([x.T for x in o_t], axis=1).astype(o_ref.dtype)


def mem_attention(qc, kv, tl_pref=512):
    B, L, _ = qc.shape
    M = kv.shape[1]
    tl = _tile(L, tl_pref)
    return pl.pallas_call(
        _mem_attn_kernel, grid=(B, L // tl),
        in_specs=[pl.BlockSpec((None, tl, MX_W), lambda b, i: (b, i, 0)),
                  pl.BlockSpec((None, M, 2 * MX_W), lambda b, i: (b, 0, 0))],
        out_specs=pl.BlockSpec((None, tl, MX_W), lambda b, i: (b, i, 0)),
        out_shape=jax.ShapeDtypeStruct((B, L, MX_W), BF16),
        compiler_params=_params("parallel", "parallel"), name="mem_attention",
    )(qc, kv)


def _out_proj_a_kernel(of_ref, ob_ref, z_ref, xo_ref, x_ref, on_ref, w_ref, o_ref):
    parts = []
    z = z_ref[...].astype(F32)
    for h in range(DN_HEADS):
        o = of_ref[h].astype(F32) + ob_ref[h].astype(F32)
        y = _rmsnorm(o, on_ref[...]) * _silu(z[:, h * DN_D:(h + 1) * DN_D])
        parts.append(y.astype(BF16))
    parts.append(xo_ref[...])
    mixed = jnp.concatenate(parts, axis=1)
    o_ref[...] = x_ref[...] + _dot(mixed, w_ref[...])


def out_proj_a(o_f, o_b, z, xo, x, o_norm, w_out, tm_pref=512):
    B, H, L, _ = o_f.shape
    D = x.shape[-1]
    tm = _tile(L, tm_pref)
    hm = pl.BlockSpec((None, H, tm, DN_D), lambda b, i: (b, 0, i, 0))
    row = lambda w: pl.BlockSpec((None, tm, w), lambda b, i: (b, i, 0))
    return pl.pallas_call(
        _out_proj_a_kernel, grid=(B, L // tm),
        in_specs=[hm, hm, row(DN_V), row(MX_W), row(D),
                  pl.BlockSpec((1, DN_D), lambda b, i: (0, 0)),
                  pl.BlockSpec(w_out.shape, lambda b, i: (0, 0))],
        out_specs=row(D), out_shape=jax.ShapeDtypeStruct((B, L, D), F32),
        compiler_params=_params("parallel", "parallel"), name="out_proj_a",
    )(o_f, o_b, z, xo, x, o_norm.reshape(1, DN_D), w_out)


def _out_proj_b_kernel(o_ref, xo_ref, x_ref, w_ref, out_ref):
    mixed = jnp.concatenate([o_ref[...], xo_ref[...]], axis=1)
    out_ref[...] = x_ref[...] + _dot(mixed, w_ref[...])


def out_proj_b(o, xo, x, w_out, tm_pref=512):
    B, L, D = x.shape
    tm = _tile(L, tm_pref)
    row = lambda w: pl.BlockSpec((None, tm, w), lambda b, i: (b, i, 0))
    return pl.pallas_call(
        _out_proj_b_kernel, grid=(B, L // tm),
        in_specs=[row(WA_Q), row(MX_W), row(D), pl.BlockSpec(w_out.shape, lambda b, i: (0, 0))],
        out_specs=row(D), out_shape=jax.ShapeDtypeStruct((B, L, D), F32),
        compiler_params=_params("parallel", "parallel"), name="out_proj_b",
    )(o, xo, x, w_out)


def _swiglu_step(h, wg_ref, wu_ref, wd_ref, acc_ref):
    gate = _dot(h, wg_ref[...])
    up = _dot(h, wu_ref[...])
    acc_ref[...] += _dot((_silu(gate) * up).astype(BF16), wd_ref[...])


def _ffn_kernel(x_ref, g_ref, wg_ref, wu_ref, wd_ref, o_ref, h_ref, acc_ref):
    j = pl.program_id(1)

    @pl.when(j == 0)
    def _():
        h_ref[...] = _rmsnorm(x_ref[...], g_ref[...]).astype(BF16)
        acc_ref[...] = jnp.zeros_like(acc_ref)

    _swiglu_step(h_ref[...], wg_ref, wu_ref, wd_ref, acc_ref)

    @pl.when(j == pl.num_programs(1) - 1)
    def _():
        o_ref[...] = x_ref[...] + acc_ref[...]


def ffn(x, g, w_gate, w_up, w_down, tm_pref=1024):
    T, D = x.shape
    F = w_gate.shape[1]
    tm = _tile(T, tm_pref)
    tf = _tile(F, FF_TILE)
    return pl.pallas_call(
        _ffn_kernel, grid=(T // tm, F // tf),
        in_specs=[pl.BlockSpec((tm, D), lambda i, j: (i, 0)),
                  pl.BlockSpec((1, D), lambda i, j: (0, 0)),
                  pl.BlockSpec((D, tf), lambda i, j: (0, j)),
                  pl.BlockSpec((D, tf), lambda i, j: (0, j)),
                  pl.BlockSpec((tf, D), lambda i, j: (j, 0))],
        out_specs=pl.BlockSpec((tm, D), lambda i, j: (i, 0)),
        out_shape=jax.ShapeDtypeStruct((T, D), F32),
        scratch_shapes=[pltpu.VMEM((tm, D), BF16), pltpu.VMEM((tm, D), F32)],
        compiler_params=_params("parallel", "arbitrary"), name="ffn",
    )(x, g.reshape(1, D), w_gate, w_up, w_down)


def _win_attn_kernel(sink_ref, q_ref, kp_ref, kc_ref, kn_ref, vp_ref, vc_ref, vn_ref, bias_ref, o_ref):
    i = pl.program_id(1)
    n = pl.num_programs(1)
    qb = q_ref.shape[0]
    q = q_ref[...]
    k = jnp.concatenate([kp_ref[...], kc_ref[...], kn_ref[...]], axis=0)
    v = jnp.concatenate([vp_ref[...], vc_ref[...], vn_ref[...]], axis=0)
    bias = bias_ref[(i == 0).astype(I32) + 2 * (i == n - 1).astype(I32)]
    bias = jnp.concatenate([bias] * WA_GROUP, axis=1)
    cols = WA_GROUP * qb
    grp = lax.broadcasted_iota(I32, (1, cols), 1) // qb
    heads = range(WA_KV_HEADS)
    hs = [slice(h * WA_HEAD_DIM, (h + 1) * WA_HEAD_DIM) for h in heads]
    qs = [jnp.concatenate(
        [q[:, (h * WA_GROUP + g) * WA_HEAD_DIM:(h * WA_GROUP + g + 1) * WA_HEAD_DIM]
         for g in range(WA_GROUP)], axis=0) for h in heads]
    sk = []
    for h in heads:
        row = jnp.zeros((1, cols), F32)
        for g in range(WA_GROUP):
            row = jnp.where(grp == g, sink_ref[h * WA_GROUP + g] * LOG2_E, row)
        sk.append(row)
    s = [_dot_nt(k[:, hs[h]], qs[h]) + bias for h in heads]
    m = [jnp.maximum(jnp.max(s[h], axis=0, keepdims=True), sk[h]) for h in heads]
    p = [jnp.exp2(s[h] - m[h]) for h in heads]
    den = [jnp.sum(p[h], axis=0, keepdims=True) + jnp.exp2(sk[h] - m[h]) for h in heads]
    o = [(_dot_tn(v[:, hs[h]], p[h].astype(BF16)) / den[h]).T for h in heads]
    outs = [o[h][g * qb:(g + 1) * qb] for h in heads for g in range(WA_GROUP)]
    o_ref[...] = jnp.concatenate(outs, axis=1).astype(o_ref.dtype)


def _window_bias(qb):
    nk = qb + 2 * WA_BLOCK
    kj = jnp.arange(nk)[:, None]
    qi = jnp.arange(qb)[None, :]
    band = jnp.abs(kj - WA_BLOCK - qi) <= WA_BLOCK
    not_prev = kj >= WA_BLOCK
    not_next = kj < WA_BLOCK + qb
    variants = [band, band & not_prev, band & not_next, band & not_prev & not_next]
    return jnp.stack([jnp.where(m, 0.0, -jnp.inf).astype(F32) for m in variants])


def window_attention(q, k, v, sink):
    B, L, _ = q.shape
    qb = _tile(L, WA_QBLOCK)
    n = L // qb
    per = qb // WA_BLOCK
    nhalo = L // WA_BLOCK
    prev = lambda b, i, s: (b, jnp.maximum(i * per - 1, 0), 0)
    cur = lambda b, i, s: (b, i, 0)
    nxt = lambda b, i, s: (b, jnp.minimum((i + 1) * per, nhalo - 1), 0)
    halo = lambda f: pl.BlockSpec((None, WA_BLOCK, WA_KV), f)
    body = pl.BlockSpec((None, qb, WA_KV), cur)
    bias = _window_bias(qb)
    grid_spec = pltpu.PrefetchScalarGridSpec(
        num_scalar_prefetch=1, grid=(B, n),
        in_specs=[pl.BlockSpec((None, qb, WA_Q), cur),
                  halo(prev), body, halo(nxt), halo(prev), body, halo(nxt),
                  pl.BlockSpec(bias.shape, lambda b, i, s: (0, 0, 0))],
        out_specs=pl.BlockSpec((None, qb, WA_Q), cur))
    return pl.pallas_call(
        _win_attn_kernel, grid_spec=grid_spec,
        out_shape=jax.ShapeDtypeStruct((B, L, WA_Q), BF16),
        compiler_params=_params("parallel", "parallel"), name="window_attention",
    )(sink, q, k, k, k, v, v, v, bias)


def _rope_tables(L):
    inv = ROPE_THETA ** (-jnp.arange(0, WA_HEAD_DIM, 2, dtype=F32) / WA_HEAD_DIM)
    ang = jnp.arange(L, dtype=F32)[:, None] * inv[None, :]
    cos, sin = jnp.cos(ang), jnp.sin(ang)
    reps = LANES // WA_HEAD_DIM
    return jnp.tile(jnp.concatenate([cos, cos], axis=1), (1, reps)), \
        jnp.tile(jnp.concatenate([-sin, sin], axis=1), (1, reps))


def _router_kernel(tm, x_ref, g_ref, wr_ref, hs_ref, mi_ref, mf_ref, cnt_ref, before_ref, carry_ref):
    @pl.when(pl.program_id(0) == 0)
    def _():
        carry_ref[...] = jnp.zeros_like(carry_ref)

    before_ref[...] = carry_ref[...]
    hf = _rmsnorm(x_ref[...], g_ref[...])
    for s in range(TOK_TILES):
        hs_ref[pl.ds(s, tm, stride=TOK_TILES), :] = hf[:, s * LANES:(s + 1) * LANES]
    logits = _dot_nt_x3(wr_ref[...], hf)
    row = lax.broadcasted_iota(I32, logits.shape, 0)
    lg = jnp.where(row < N_EXPERTS, logits, -jnp.inf)
    m1 = jnp.max(lg, axis=0, keepdims=True)
    i1 = jnp.min(jnp.where(lg == m1, row, 2 * N_EXPERTS), axis=0, keepdims=True)
    lg2 = jnp.where(row == i1, -jnp.inf, lg)
    m2 = jnp.max(lg2, axis=0, keepdims=True)
    i2 = jnp.min(jnp.where(lg2 == m2, row, 2 * N_EXPERTS), axis=0, keepdims=True)
    e2 = jnp.exp(m2 - m1)
    g1 = 1.0 / (1.0 + e2)
    g2 = e2 / (1.0 + e2)
    oh1 = row == i1
    oh2 = row == i2
    oh = (oh1 | oh2).astype(F32)
    r = lax.broadcasted_iota(I32, (tm, tm), 0)
    c = lax.broadcasted_iota(I32, (tm, tm), 1)
    before = _dot(oh.astype(BF16), (r < c).astype(BF16)) + carry_ref[:, 0:1]
    rank1 = jnp.sum(jnp.where(oh1, before, 0.0), axis=0, keepdims=True).astype(I32)
    rank2 = jnp.sum(jnp.where(oh2, before, 0.0), axis=0, keepdims=True).astype(I32)
    carry_ref[...] += jnp.sum(oh, axis=1, keepdims=True)
    out_row = lax.broadcasted_iota(I32, mi_ref.shape, 0)
    mi_ref[...] = jnp.where(out_row == 0, i1, jnp.where(out_row == 1, i2, jnp.where(out_row == 2, rank1,
                            jnp.where(out_row == 3, rank2, 0))))
    mf_ref[...] = jnp.where(out_row == 0, g1, jnp.where(out_row == 1, g2, 0.0))
    cnt_ref[...] = carry_ref[...]


def moe_router(x, g, w_router, tm):
    T, D = x.shape
    rows = 2 * N_EXPERTS
    wr = jnp.zeros((rows, D), F32).at[:N_EXPERTS].set(w_router.T)
    return pl.pallas_call(
        functools.partial(_router_kernel, tm), grid=(T // tm,),
        in_specs=[pl.BlockSpec((tm, D), lambda i: (i, 0)),
                  pl.BlockSpec((1, D), lambda i: (0, 0)),
                  pl.BlockSpec((rows, D), lambda i: (0, 0))],
        out_specs=[pl.BlockSpec((tm * TOK_TILES, LANES), lambda i: (i, 0)),
                   pl.BlockSpec((SUBLANES, tm), lambda i: (0, i)),
                   pl.BlockSpec((SUBLANES, tm), lambda i: (0, i)),
                   pl.BlockSpec((rows, LANES), lambda i: (0, 0)),
                   pl.BlockSpec((rows, LANES), lambda i: (i, 0))],
        out_shape=[jax.ShapeDtypeStruct((T * TOK_TILES, LANES), F32),
                   jax.ShapeDtypeStruct((SUBLANES, T), I32),
                   jax.ShapeDtypeStruct((SUBLANES, T), F32),
                   jax.ShapeDtypeStruct((rows, LANES), F32),
                   jax.ShapeDtypeStruct((T // tm * rows, LANES), F32)],
        scratch_shapes=[pltpu.VMEM((rows, LANES), F32)],
        compiler_params=_params("arbitrary"), name="moe_router",
    )(x, g.reshape(1, D), wr)


def _row_copy(src, src_row, dst, dst_row, sem):
    return pltpu.make_async_copy(src.at[pl.ds(src_row * TOK_TILES, TOK_TILES), :],
                                 dst.at[pl.ds(dst_row * TOK_TILES, TOK_TILES), :], sem)


def _segment_copies(seg_ref, tile, stage_ref, hbm_ref, sem, to_hbm, wait):
    for e in range(N_EXPERTS):
        col = tile * N_EXPERTS + e
        row0, n, off = seg_ref[0, col], seg_ref[1, col], seg_ref[2, col]

        def piece(first, rows):
            a = stage_ref.at[pl.ds((off + first) * TOK_TILES, rows * TOK_TILES), :]
            b = hbm_ref.at[pl.ds((row0 + first) * TOK_TILES, rows * TOK_TILES), :]
            cp = pltpu.make_async_copy(a, b, sem) if to_hbm else pltpu.make_async_copy(b, a, sem)
            cp.wait() if wait else cp.start()

        def full(c, carry):
            piece(c * SEG_CHUNK, SEG_CHUNK)
            return carry

        lax.fori_loop(0, n // SEG_CHUNK, full, 0)
        rem = n % SEG_CHUNK
        for rows in (4, 2, 1):
            covered = (n - rem) + (rem - rem % (2 * rows))
            pl.when((rem & rows) != 0)(functools.partial(piece, covered, rows))


def _dispatch_kernel(tm, seg_ref, tail_ref, pos_ref, hs_ref, xs_ref, stage0_ref, stage1_ref, zero_ref,
                     sem0, sem1, zero_sem):
    i = pl.program_id(0)
    last = pl.num_programs(0) - 1

    @pl.when(i == 0)
    def _():
        zero_ref[...] = jnp.zeros_like(zero_ref)
        for wait in (False, True):
            for e in range(N_EXPERTS):
                def pad_row(r, carry):
                    cp = _row_copy(zero_ref, 0, xs_ref, tail_ref[0, e] + r, zero_sem)
                    cp.wait() if wait else cp.start()
                    return carry

                lax.fori_loop(0, tail_ref[1, e], pad_row, 0)

            def pad_chunk(r, carry):
                row = tail_ref[0, N_EXPERTS] + r * ZERO_ROWS
                cp = pltpu.make_async_copy(
                    zero_ref, xs_ref.at[pl.ds(row * TOK_TILES, ZERO_ROWS * TOK_TILES), :], zero_sem)
                cp.wait() if wait else cp.start()
                return carry

            lax.fori_loop(0, tail_ref[1, N_EXPERTS], pad_chunk, 0)

    def run(stage_ref, sem, other_ref, other_sem):
        @pl.when(i >= 2)
        def _():
            _segment_copies(seg_ref, i - 2, stage_ref, xs_ref, sem, True, True)

        def place(t, carry):
            row = hs_ref[pl.ds(t * TOK_TILES, TOK_TILES), :]
            for k in range(TOP_K):
                stage_ref[pl.ds(pos_ref[k, t] * TOK_TILES, TOK_TILES), :] = row
            return carry

        lax.fori_loop(0, tm, place, 0, unroll=8)
        _segment_copies(seg_ref, i, stage_ref, xs_ref, sem, True, False)

        @pl.when(i == last)
        def _():
            @pl.when(i >= 1)
            def _():
                _segment_copies(seg_ref, i - 1, other_ref, xs_ref, other_sem, True, True)

            _segment_copies(seg_ref, i, stage_ref, xs_ref, sem, True, True)

    pl.when(i % 2 == 0)(functools.partial(run, stage0_ref, sem0, stage1_ref, sem1))
    pl.when(i % 2 == 1)(functools.partial(run, stage1_ref, sem1, stage0_ref, sem0))


def moe_dispatch(hs, pos, seg, tail, n_rows, tm):
    T = pos.shape[1]
    stage = pltpu.VMEM((TOP_K * tm * TOK_TILES, LANES), F32)
    grid_spec = pltpu.PrefetchScalarGridSpec(
        num_scalar_prefetch=2, grid=(T // tm,),
        in_specs=[pl.BlockSpec((TOP_K, tm), lambda i, sg, tl: (0, i), memory_space=pltpu.SMEM),
                  pl.BlockSpec((tm * TOK_TILES, LANES), lambda i, sg, tl: (i, 0))],
        out_specs=pl.BlockSpec(memory_space=pl.ANY),
        scratch_shapes=[stage, stage, pltpu.VMEM((ZERO_ROWS * TOK_TILES, LANES), F32),
                        pltpu.SemaphoreType.DMA(()), pltpu.SemaphoreType.DMA(()),
                        pltpu.SemaphoreType.DMA(())])
    return pl.pallas_call(
        functools.partial(_dispatch_kernel, tm), grid_spec=grid_spec,
        out_shape=jax.ShapeDtypeStruct((n_rows * TOK_TILES, LANES), F32),
        compiler_params=_params("arbitrary"), name="moe_dispatch",
    )(seg, tail, pos, hs)


def _expert_kernel(tm, be_ref, nu_ref, xs_ref, wg_ref, wu_ref, wd_ref, ys_ref, h_ref, acc_ref):
    b = pl.program_id(0)
    j = pl.program_id(1)
    last = pl.num_programs(1) - 1
    used = b < nu_ref[0]

    @pl.when(used & (j == 0))
    def _():
        for s in range(TOK_TILES):
            h_ref[:, s * LANES:(s + 1) * LANES] = xs_ref[pl.ds(s, tm, stride=TOK_TILES), :].astype(BF16)
        acc_ref[...] = jnp.zeros_like(acc_ref)

    @pl.when(used)
    def _():
        _swiglu_step(h_ref[...], wg_ref, wu_ref, wd_ref, acc_ref)

    @pl.when(used & (j == last))
    def _():
        for s in range(TOK_TILES):
            ys_ref[pl.ds(s, tm, stride=TOK_TILES), :] = acc_ref[:, s * LANES:(s + 1) * LANES]

    @pl.when(jnp.logical_not(used) & (j == last))
    def _():
        ys_ref[...] = jnp.zeros_like(ys_ref)


def moe_experts(xs, blk_expert, n_used, w_gate, w_up, w_down, tm):
    E, D, F = w_gate.shape
    nb = xs.shape[0] // (tm * TOK_TILES)
    tf = _tile(F, FF_TILE)
    nj = F // tf
    jj = lambda b, j, be, nu: jnp.where(b < nu[0], j, nj - 1)
    grid_spec = pltpu.PrefetchScalarGridSpec(
        num_scalar_prefetch=2, grid=(nb, nj),
        in_specs=[pl.BlockSpec((tm * TOK_TILES, LANES), lambda b, j, be, nu: (jnp.minimum(b, nu[0] - 1), 0)),
                  pl.BlockSpec((None, D, tf), lambda b, j, be, nu: (be[b], 0, jj(b, j, be, nu))),
                  pl.BlockSpec((None, D, tf), lambda b, j, be, nu: (be[b], 0, jj(b, j, be, nu))),
                  pl.BlockSpec((None, tf, D), lambda b, j, be, nu: (be[b], jj(b, j, be, nu), 0))],
        out_specs=pl.BlockSpec((tm * TOK_TILES, LANES), lambda b, j, be, nu: (b, 0)),
        scratch_shapes=[pltpu.VMEM((tm, D), BF16), pltpu.VMEM((tm, D), F32)])
    return pl.pallas_call(
        functools.partial(_expert_kernel, tm), grid_spec=grid_spec,
        out_shape=jax.ShapeDtypeStruct(xs.shape, F32),
        compiler_params=_params("arbitrary", "arbitrary"), name="moe_experts",
    )(blk_expert, n_used, xs, w_gate, w_up, w_down)


def _combine_kernel(tm, seg_ref, pos_ref, gate_ref, ys_ref, x_ref, g_ref, o_ref,
                    stage0_ref, stage1_ref, ybuf_ref, y_ref, sem0, sem1):
    i = pl.program_id(0)
    n = pl.num_programs(0)

    def run(stage_ref, sem, other_ref, other_sem):
        @pl.when(i == 0)
        def _():
            _segment_copies(seg_ref, i, stage_ref, ys_ref, sem, False, False)

        @pl.when(i + 1 < n)
        def _():
            _segment_copies(seg_ref, i + 1, other_ref, ys_ref, other_sem, False, False)

        _segment_copies(seg_ref, i, stage_ref, ys_ref, sem, False, True)

        def pick(t, carry):
            a = stage_ref[pl.ds(pos_ref[0, t] * TOK_TILES, TOK_TILES), :]
            b = stage_ref[pl.ds(pos_ref[1, t] * TOK_TILES, TOK_TILES), :]
            ybuf_ref[pl.ds(t * TOK_TILES, TOK_TILES), :] = a * gate_ref[0, t] + b * gate_ref[1, t]
            return carry

        lax.fori_loop(0, tm, pick, 0, unroll=8)

    pl.when(i % 2 == 0)(functools.partial(run, stage0_ref, sem0, stage1_ref, sem1))
    pl.when(i % 2 == 1)(functools.partial(run, stage1_ref, sem1, stage0_ref, sem0))
    for s in range(TOK_TILES):
        y_ref[:, s * LANES:(s + 1) * LANES] = ybuf_ref[pl.ds(s, tm, stride=TOK_TILES), :]
    o_ref[...] = _rmsnorm(x_ref[...] + y_ref[...], g_ref[...])


def moe_combine(ys, pos, gates, seg, x, g_final, tm):
    T, D = x.shape
    stage = pltpu.VMEM((TOP_K * tm * TOK_TILES, LANES), F32)
    grid_spec = pltpu.PrefetchScalarGridSpec(
        num_scalar_prefetch=1, grid=(T // tm,),
        in_specs=[pl.BlockSpec((TOP_K, tm), lambda i, sg: (0, i), memory_space=pltpu.SMEM),
                  pl.BlockSpec((TOP_K, tm), lambda i, sg: (0, i), memory_space=pltpu.SMEM),
                  pl.BlockSpec(memory_space=pl.ANY),
                  pl.BlockSpec((tm, D), lambda i, sg: (i, 0)),
                  pl.BlockSpec((1, D), lambda i, sg: (0, 0))],
        out_specs=pl.BlockSpec((tm, D), lambda i, sg: (i, 0)),
        scratch_shapes=[stage, stage, pltpu.VMEM((tm * TOK_TILES, LANES), F32), pltpu.VMEM((tm, D), F32),
                        pltpu.SemaphoreType.DMA(()), pltpu.SemaphoreType.DMA(())])
    return pl.pallas_call(
        functools.partial(_combine_kernel, tm), grid_spec=grid_spec,
        out_shape=jax.ShapeDtypeStruct((T, D), F32),
        compiler_params=_params("arbitrary"), name="moe_combine",
    )(seg, pos, gates, ys, x, g_final.reshape(1, D))


def moe_layer(x, g_ffn, g_final, w_router, w_gate, w_up, w_down):
    T = x.shape[0]
    tm = MOE_ROWS_LARGE if T * TOP_K >= MOE_ROWS_LARGE * N_EXPERTS * 8 else MOE_ROWS_SMALL
    tt = _tile(T, MOE_TILE)
    hs, meta_i, meta_f, cnt, before = moe_router(x, g_ffn, w_router, tt)
    counts = cnt[:N_EXPERTS, 0].astype(I32)
    padded = (counts + tm - 1) // tm * tm
    pad_end = jnp.cumsum(padded)
    pad_start = pad_end - padded
    idx = meta_i[0:TOP_K]
    rank = meta_i[TOP_K:2 * TOP_K]
    before = before.reshape(T // tt, 2 * N_EXPERTS, LANES)[:, :N_EXPERTS, 0].astype(I32)
    seg_len = jnp.concatenate([before[1:], counts[None]], axis=0) - before
    stage_off = jnp.cumsum(seg_len, axis=1) - seg_len
    seg = jnp.stack([(pad_start[None] + before).reshape(-1), seg_len.reshape(-1),
                     stage_off.reshape(-1)]).astype(I32)
    in_tile = jnp.repeat(stage_off - before, tt, axis=0)
    chosen = idx[..., None] == jnp.arange(N_EXPERTS, dtype=I32)
    pos = (rank + jnp.sum(jnp.where(chosen, in_tile[None], 0), axis=-1)).astype(I32)
    gates = meta_f[0:TOP_K]
    nb = (T * TOP_K) // tm + N_EXPERTS
    n_used = (pad_end[-1] // tm).astype(I32)
    blk_row = jnp.minimum(jnp.arange(nb, dtype=I32), n_used - 1) * tm
    blk_expert = jnp.minimum(jnp.searchsorted(pad_end, blk_row, side='right'), N_EXPERTS - 1).astype(I32)
    tail = jnp.stack([jnp.append(pad_start + counts, pad_end[-1]),
                      jnp.append(padded - counts, (nb * tm - pad_end[-1]) // ZERO_ROWS)]).astype(I32)
    xs = moe_dispatch(hs, pos, seg, tail, nb * tm, tt)
    ys = moe_experts(xs, blk_expert, n_used.reshape(1), w_gate, w_up, w_down, tm)
    return moe_combine(ys, pos, gates, seg, x, g_final, tt)


def _prep_weights(g_mix, g_mem, g_ffn, g_final, a_w_in, a_conv, a_log, a_dt_bias, a_o_norm,
                  a_mem_kv, a_w_out, b_w_in, b_sink, b_mem_kv, b_w_out, f_w_gate, f_w_up, f_w_down,
                  e_router, e_w_gate, e_w_up, e_w_down):
    s1 = DN_QKV
    s2 = s1 + DN_V
    s3 = s2 + 4 * DN_HEADS
    w = a_w_in[0]
    gates_w = jnp.zeros((GATE_ROWS, D_MODEL), F32).at[:4 * DN_HEADS].set(w[:, s2:s3].T)
    wb = b_w_in[0]
    return dict(
        g_mix=g_mix, g_mem=g_mem, g_ffn=g_ffn, g_final=g_final,
        a_in=[w[:, :s1].astype(BF16), w[:, s1:s2].astype(BF16), w[:, s3:].astype(BF16), gates_w],
        a_conv=a_conv[0], a_log=a_log[0], a_dt_bias=a_dt_bias[0], a_o_norm=a_o_norm[0],
        a_mem_kv=a_mem_kv[0].astype(BF16), a_w_out=a_w_out[0].astype(BF16),
        b_in=[wb[:, :WA_Q].astype(BF16), wb[:, WA_Q:WA_Q + WA_KV].astype(BF16),
              wb[:, WA_Q + WA_KV:WA_Q + 2 * WA_KV].astype(BF16), wb[:, WA_Q + 2 * WA_KV:].astype(BF16)],
        b_sink=b_sink[0], b_mem_kv=b_mem_kv[0].astype(BF16), b_w_out=b_w_out[0].astype(BF16),
        f_w_gate=f_w_gate[0].astype(BF16), f_w_up=f_w_up[0].astype(BF16), f_w_down=f_w_down[0].astype(BF16),
        e_router=e_router[0], e_w_gate=e_w_gate[0].astype(BF16), e_w_up=e_w_up[0].astype(BF16),
        e_w_down=e_w_down[0].astype(BF16))


def _trunk(x, mem, p):
    B, L, D = x.shape
    M = mem.shape[1]
    T = B * L
    x2 = x.reshape(T, D)
    mem2 = mem.reshape(B * M, D)

    qkv, z, qc, gates = norm_proj(x2, p['g_mix'][0], p['a_in'], [BF16, BF16, BF16, F32],
                                  scales=(None, None, MX_HEAD_DIM ** -0.5 * LOG2_E, None))
    (kv,) = norm_proj(mem2, p['g_mem'][0], [p['a_mem_kv']], [BF16])
    q, k, v, pack, packt = dn_prep(qkv.reshape(B, L, DN_QKV), gates,
                                   p['a_conv'], p['a_log'], p['a_dt_bias'])
    o_f, o_b = delta_rule(q, k, v, pack, packt)
    xo = mem_attention(qc.reshape(B, L, MX_W), kv.reshape(B, M, 2 * MX_W))
    x1 = out_proj_a(o_f, o_b, z.reshape(B, L, DN_V), xo, x, p['a_o_norm'], p['a_w_out'])
    x1 = ffn(x1.reshape(T, D), p['g_ffn'][0], p['f_w_gate'], p['f_w_up'], p['f_w_down'])

    cos, sin_signed = _rope_tables(L)
    q, k, v, qc = norm_proj(x1, p['g_mix'][1], p['b_in'], [BF16, BF16, BF16, BF16],
                            scales=(WA_HEAD_DIM ** -0.5 * LOG2_E, None, None, MX_HEAD_DIM ** -0.5 * LOG2_E),
                            rope=(cos, sin_signed, (True, True, False, False)))
    (kv,) = norm_proj(mem2, p['g_mem'][1], [p['b_mem_kv']], [BF16])
    o = window_attention(q.reshape(B, L, WA_Q), k.reshape(B, L, WA_KV), v.reshape(B, L, WA_KV),
                         p['b_sink'])
    xo = mem_attention(qc.reshape(B, L, MX_W), kv.reshape(B, M, 2 * MX_W))
    x2 = out_proj_b(o, xo, x1.reshape(B, L, D), p['b_w_out'])
    y = moe_layer(x2.reshape(T, D), p['g_ffn'][1], p['g_final'], p['e_router'],
                  p['e_w_gate'], p['e_w_up'], p['e_w_down'])
    return y.reshape(B, L, D)


def kernel(x_prompt, x_sample, mem_prompt, mem_sample, g_mix, g_mem, g_ffn, g_final, a_w_in, a_conv,
           a_log, a_dt_bias, a_o_norm, a_mem_kv, a_w_out, b_w_in, b_sink, b_mem_kv, b_w_out,
           f_w_gate, f_w_up, f_w_down, e_router, e_w_gate, e_w_up, e_w_down):
    p = _prep_weights(g_mix, g_mem, g_ffn, g_final, a_w_in, a_conv, a_log, a_dt_bias, a_o_norm,
                      a_mem_kv, a_w_out, b_w_in, b_sink, b_mem_kv, b_w_out, f_w_gate, f_w_up, f_w_down,
                      e_router, e_w_gate, e_w_up, e_w_down)
    return (_trunk(x_prompt, mem_prompt, p), _trunk(x_sample, mem_sample, p))
```

```python
import functools
import math

import jax
import jax.numpy as jnp
from jax import lax
from jax.experimental import pallas as pl
from jax.experimental.pallas import tpu as pltpu

F32 = jnp.float32
BF16 = jnp.bfloat16
I32 = jnp.int32

EPS = 1e-6
LOG2_E = math.log2(math.e)
D_MODEL = 1024
LANES = 128
SUBLANES = 8
TOK_TILES = D_MODEL // LANES
VMEM_LIMIT_BYTES = 56 * 1024 * 1024

DN_HEADS = 6
DN_D = 128
DN_CONV = 5
DN_CHUNK = 64
DN_QKV = 3 * DN_HEADS * DN_D
DN_V = DN_HEADS * DN_D
DN_TILE = 256
CONV_HALO = 16
GATE_ROWS = 48

WA_Q_HEADS = 12
WA_KV_HEADS = 4
WA_GROUP = WA_Q_HEADS // WA_KV_HEADS
WA_HEAD_DIM = 64
WA_BLOCK = 128
WA_QBLOCK = 256
WA_Q = WA_Q_HEADS * WA_HEAD_DIM
WA_KV = WA_KV_HEADS * WA_HEAD_DIM
ROPE_THETA = 10000.0

MX_HEADS = 4
MX_HEAD_DIM = 64
MX_W = MX_HEADS * MX_HEAD_DIM

D_FF = 3584
N_EXPERTS = 8
TOP_K = 2
FF_TILE = 512
MOE_TILE = 512
MOE_ROWS_SMALL = 512
MOE_ROWS_LARGE = 1024
ZERO_ROWS = 64
SEG_CHUNK = 8


def _params(*sem):
    return pltpu.CompilerParams(dimension_semantics=sem, vmem_limit_bytes=VMEM_LIMIT_BYTES)


def _tile(n, pref):
    t = min(n, pref)
    assert n % t == 0, (n, pref)
    return t


def _dot(a, b):
    return jnp.dot(a, b, preferred_element_type=F32)


def _dot_nt(a, b):
    return lax.dot_general(a, b, (((1,), (1,)), ((), ())), preferred_element_type=F32)


def _dot_tn(a, b):
    return lax.dot_general(a, b, (((0,), (0,)), ((), ())), preferred_element_type=F32)


def _split3(x):
    hi = x.astype(BF16)
    r1 = x - hi.astype(F32)
    mid = r1.astype(BF16)
    lo = (r1 - mid.astype(F32)).astype(BF16)
    return hi, mid, lo


def _dot_nt_x3(w, h):
    w_hi = w.astype(BF16)
    w_lo = (w - w_hi.astype(F32)).astype(BF16)
    h_hi = h.astype(BF16)
    h_lo = (h - h_hi.astype(F32)).astype(BF16)
    return _dot_nt(w_hi, h_hi) + _dot_nt(w_lo, h_hi) + _dot_nt(w_hi, h_lo)


def _rmsnorm(x, g):
    ms = jnp.mean(x * x, axis=-1, keepdims=True)
    return x * lax.rsqrt(ms + EPS) * g


def _silu(x):
    return x * jax.nn.sigmoid(x)


def _rope(x, cos, sin_signed):
    n, w = x.shape
    lane = lax.broadcasted_iota(I32, (n, LANES), 1)
    first_half = (lane % WA_HEAD_DIM) < (WA_HEAD_DIM // 2)
    outs = []
    for gidx in range(w // LANES):
        xs = x[:, gidx * LANES:(gidx + 1) * LANES]
        rot = jnp.where(first_half, pltpu.roll(xs, LANES - WA_HEAD_DIM // 2, 1),
                        pltpu.roll(xs, WA_HEAD_DIM // 2, 1))
        outs.append(xs * cos + rot * sin_signed)
    return jnp.concatenate(outs, axis=1)


def _norm_proj_kernel(n_out, scales, rope_flags, x_ref, g_ref, *refs):
    if rope_flags is not None:
        cos_ref, sin_ref, refs = refs[0], refs[1], refs[2:]
    w_refs, o_refs = refs[:n_out], refs[n_out:]
    hf = _rmsnorm(x_ref[...], g_ref[...])
    hb = hf.astype(BF16)
    for idx, (w_ref, o_ref) in enumerate(zip(w_refs, o_refs)):
        if w_ref.dtype == F32:
            y = _dot_nt_x3(w_ref[...], hf)
        else:
            y = _dot(hb, w_ref[...])
        if rope_flags is not None and rope_flags[idx]:
            y = _rope(y, cos_ref[...], sin_ref[...])
        if scales is not None and scales[idx] is not None:
            y = y * scales[idx]
        o_ref[...] = y.astype(o_ref.dtype)


def norm_proj(x, g, weights, out_dtypes, scales=None, rope=None, tm_pref=512):
    T, D = x.shape
    tm = _tile(T, tm_pref)
    n_out = len(weights)
    args = [x, g.reshape(1, D)]
    extra_specs = []
    flags = None
    if rope is not None:
        cos, sin_signed, flags = rope
        tm = _tile(cos.shape[0], tm)
        per_seq = cos.shape[0] // tm
        extra_specs = [pl.BlockSpec((tm, LANES), lambda i: (i % per_seq, 0))] * 2
        args += [cos, sin_signed]
    in_specs = [pl.BlockSpec((tm, D), lambda i: (i, 0)), pl.BlockSpec((1, D), lambda i: (0, 0))]
    in_specs += extra_specs + [pl.BlockSpec(w.shape, lambda i: (0, 0)) for w in weights]
    out_specs = [pl.BlockSpec((w.shape[0], tm), lambda i: (0, i)) if w.dtype == F32 else
                 pl.BlockSpec((tm, w.shape[1]), lambda i: (i, 0)) for w in weights]
    out_shape = [jax.ShapeDtypeStruct((w.shape[0], T) if w.dtype == F32 else (T, w.shape[1]), dt)
                 for w, dt in zip(weights, out_dtypes)]
    return pl.pallas_call(
        functools.partial(_norm_proj_kernel, n_out, scales, flags),
        grid=(T // tm,), in_specs=in_specs, out_specs=out_specs, out_shape=out_shape,
        compiler_params=_params("parallel"), name="norm_proj",
    )(*args, *weights)


def _dn_prep_kernel(tl, cur_ref, prev_ref, next_ref, gates_ref, cw_ref, alog_ref, dtb_ref, shift_ref,
                    q_ref, k_ref, v_ref, pack_ref, packt_ref):
    i = pl.program_id(1)
    nt = pl.num_programs(1)
    pad = (DN_CONV - 1) // 2
    taps = [j for j in range(DN_CONV) if j != pad]
    p8 = prev_ref[CONV_HALO - SUBLANES:, :].astype(F32) * (i > 0).astype(F32)
    n8 = next_ref[0:SUBLANES, :].astype(F32) * (i < nt - 1).astype(F32)
    row8 = lax.broadcasted_iota(I32, (SUBLANES, 1), 0)
    first = None
    last = None
    for j in range(DN_CONV):
        off = j - pad
        if off < 0:
            t = jnp.where(row8 < -off, pltpu.roll(p8, -off, 0), 0.0) * cw_ref[j:j + 1, :]
            first = t if first is None else first + t
        elif off > 0:
            t = jnp.where(row8 >= SUBLANES - off, pltpu.roll(n8, SUBLANES - off, 0), 0.0) * cw_ref[j:j + 1, :]
            last = t if last is None else last + t
    mxu_cols = 2 * LANES
    moved = None
    for cg in range(DN_QKV // LANES):
        sl = slice(cg * LANES, (cg + 1) * LANES)
        if cg % 2 == 0:
            x16 = cur_ref[:, cg * LANES:cg * LANES + mxu_cols]
            moved = {j: _dot(shift_ref[n], x16) for n, j in enumerate(taps)}
            moved[pad] = x16.astype(F32)
        half = slice((cg % 2) * LANES, (cg % 2 + 1) * LANES)
        acc = None
        for j in range(DN_CONV):
            term = moved[j][:, half] * cw_ref[j:j + 1, sl]
            acc = term if acc is None else acc + term
        acc = acc + jnp.concatenate(
            [first[:, sl], jnp.zeros((tl - 2 * SUBLANES, LANES), F32), last[:, sl]], axis=0)
        y = _silu(acc)
        if cg < DN_HEADS:
            y = y * lax.rsqrt(jnp.sum(y * y, axis=-1, keepdims=True) + EPS) * (DN_D ** -0.5)
            q_ref[cg] = y.astype(q_ref.dtype)
        elif cg < 2 * DN_HEADS:
            y = y * lax.rsqrt(jnp.sum(y * y, axis=-1, keepdims=True) + EPS)
            k_ref[cg - DN_HEADS] = y.astype(k_ref.dtype)
        else:
            v_ref[cg - 2 * DN_HEADS] = y.astype(v_ref.dtype)

    gt = gates_ref[...]
    row = lax.broadcasted_iota(I32, gt.shape, 0)
    beta = jax.nn.sigmoid(gt)
    xg = gt + dtb_ref[...]
    softplus = jnp.maximum(xg, 0.0) + jnp.log1p(jnp.exp(-jnp.abs(xg)))
    g = -jnp.exp(alog_ref[...]) * softplus
    g = jnp.where((row >= 2 * DN_HEADS) & (row < 4 * DN_HEADS), g, 0.0)
    r = lax.broadcasted_iota(I32, (tl, tl), 0)
    c = lax.broadcasted_iota(I32, (tl, tl), 1)
    same = (r // DN_CHUNK) == (c // DN_CHUNK)
    m_fwd = (same & (r <= c)).astype(BF16)
    m_bwd = (same & (r >= c)).astype(BF16)
    m_all = same.astype(BF16)
    hi, mid, lo = _split3(g)
    g_fwd = _dot(hi, m_fwd) + _dot(mid, m_fwd) + _dot(lo, m_fwd)
    g_bwd = _dot(hi, m_bwd) + _dot(mid, m_bwd) + _dot(lo, m_bwd)
    g_tot = pltpu.roll(_dot(hi, m_all) + _dot(mid, m_all) + _dot(lo, m_all), 2 * DN_HEADS, 0)
    packt = jnp.where(row < 2 * DN_HEADS, beta,
                      jnp.where(row < 3 * DN_HEADS, g_fwd,
                                jnp.where(row < 4 * DN_HEADS, g_bwd,
                                          jnp.where(row < 6 * DN_HEADS, g_tot, 0.0))))
    packt = jnp.concatenate([packt, jnp.zeros((LANES - GATE_ROWS, tl), F32)], axis=0)
    packt_ref[...] = packt
    pack_ref[...] = packt.T


def dn_prep(qkv, gates, conv_w, a_log, dt_bias):
    B, L, _ = qkv.shape
    tl = _tile(L, DN_TILE)
    nt = L // tl
    hb = tl // CONV_HALO
    nhalo = L // CONV_HALO
    cw = jnp.zeros((SUBLANES, DN_QKV), F32).at[:DN_CONV].set(conv_w)
    alog = jnp.zeros((GATE_ROWS, 1), F32).at[2 * DN_HEADS:4 * DN_HEADS, 0].set(a_log.reshape(-1))
    dtb = jnp.zeros((GATE_ROWS, 1), F32).at[2 * DN_HEADS:4 * DN_HEADS, 0].set(dt_bias.reshape(-1))
    pad = (DN_CONV - 1) // 2
    t_idx = jnp.arange(tl)
    shift = jnp.stack([(t_idx[None, :] == t_idx[:, None] + (j - pad)).astype(BF16)
                       for j in range(DN_CONV) if j != pad])
    hm = jax.ShapeDtypeStruct((B, DN_HEADS, L, DN_D), BF16)
    hm_spec = pl.BlockSpec((None, DN_HEADS, tl, DN_D), lambda b, i: (b, 0, i, 0))
    return pl.pallas_call(
        functools.partial(_dn_prep_kernel, tl),
        grid=(B, nt),
        in_specs=[
            pl.BlockSpec((None, tl, DN_QKV), lambda b, i: (b, i, 0)),
            pl.BlockSpec((None, CONV_HALO, DN_QKV), lambda b, i: (b, jnp.maximum(i * hb - 1, 0), 0)),
            pl.BlockSpec((None, CONV_HALO, DN_QKV), lambda b, i: (b, jnp.minimum((i + 1) * hb, nhalo - 1), 0)),
            pl.BlockSpec((GATE_ROWS, tl), lambda b, i: (0, b * nt + i)),
            pl.BlockSpec((SUBLANES, DN_QKV), lambda b, i: (0, 0)),
            pl.BlockSpec((GATE_ROWS, 1), lambda b, i: (0, 0)),
            pl.BlockSpec((GATE_ROWS, 1), lambda b, i: (0, 0)),
            pl.BlockSpec(shift.shape, lambda b, i: (0, 0, 0)),
        ],
        out_specs=[hm_spec, hm_spec, hm_spec,
                   pl.BlockSpec((None, tl, LANES), lambda b, i: (b, i, 0)),
                   pl.BlockSpec((None, LANES, tl), lambda b, i: (b, 0, i))],
        out_shape=[hm, hm, hm, jax.ShapeDtypeStruct((B, L, LANES), F32),
                   jax.ShapeDtypeStruct((B, LANES, L), F32)],
        compiler_params=_params("parallel", "parallel"), name="dn_prep",
    )(qkv, qkv, qkv, gates, cw, alog, dtb, shift)


def _dn_tile_stage(units, tl, refs_f, refs_b, masks_f, masks_b, u_ref, wq_ref, kd_ref, aqk_ref):
    n = len(units)
    nchunk = tl // DN_CHUNK
    refs = [refs_f if d == 0 else refs_b for d, _ in units]
    masks = [masks_f if d == 0 else masks_b for d, _ in units]
    col = []
    for (d, h), (q_ref, k_ref, v_ref, pack_ref, packt_ref) in zip(units, refs):
        pick = lambda li: pack_ref[:, li:li + 1]
        li_g = (2 + d) * DN_HEADS + h
        col.append((pick(d * DN_HEADS + h), pick(li_g), pick((4 + d) * DN_HEADS + h),
                    packt_ref[li_g:li_g + 1, :]))
    q16 = [r[0][h] for (d, h), r in zip(units, refs)]
    k16 = [r[1][h] for (d, h), r in zip(units, refs)]
    kf = [x.astype(F32) for x in k16]
    kb = [kf[i] * col[i][0] for i in range(n)]
    decay = [jnp.exp(jnp.where(masks[i][0], col[i][1] - col[i][3], -jnp.inf)) for i in range(n)]
    kk = [_dot_nt(kb[i].astype(BF16), k16[i]) for i in range(n)]
    qk = [_dot_nt(q16[i], k16[i]) for i in range(n)]
    a = [jnp.where(masks[i][1], kk[i] * decay[i], 0.0) for i in range(n)]
    for i, (d, h) in enumerate(units):
        a_qk = (qk[i] * decay[i]).astype(BF16)
        for c in range(nchunk):
            rs = slice(c * DN_CHUNK, (c + 1) * DN_CHUNK)
            aqk_ref[d * DN_HEADS + h, rs, :] = a_qk[rs, rs]
    same, eye_wide = masks[0][2], masks[0][3]

    def wide(x):
        out = x[0:DN_CHUNK]
        for c in range(1, nchunk):
            out = out + x[c * DN_CHUNK:(c + 1) * DN_CHUNK]
        return out

    def block_diag(x16):
        return jnp.where(same, jnp.concatenate([x16] * nchunk, axis=0), jnp.zeros((), BF16))

    t_w = [eye_wide - wide(x) for x in a]
    p_w16 = [wide(x).astype(BF16) for x in a]
    p_bd16 = [x.astype(BF16) for x in a]
    for it in range(int(math.log2(DN_CHUNK)) - 1):
        p_w16 = [_dot(p_w16[i], p_bd16[i]).astype(BF16) for i in range(n)]
        p_bd16 = [block_diag(x) for x in p_w16]
        t_w = [t_w[i] + _dot(t_w[i].astype(BF16), p_bd16[i]) for i in range(n)]
    t_bd16 = [block_diag(x.astype(BF16)) for x in t_w]
    e_g = [jnp.exp(col[i][1]) for i in range(n)]
    rhs = [jnp.concatenate([refs[i][2][units[i][1]].astype(F32) * col[i][0], kb[i] * e_g[i]],
                           axis=1).astype(BF16) for i in range(n)]
    uw = [_dot(t_bd16[i], rhs[i]) for i in range(n)]
    for i, (d, h) in enumerate(units):
        ui = d * DN_HEADS + h
        u_ref[ui] = uw[i][:, :DN_D]
        w16 = uw[i][:, DN_D:].astype(BF16)
        qd16 = (q16[i].astype(F32) * e_g[i]).astype(BF16)
        for c in range(nchunk):
            rs = slice(c * DN_CHUNK, (c + 1) * DN_CHUNK)
            wq_ref[ui, 2 * c * DN_CHUNK:(2 * c + 1) * DN_CHUNK, :] = w16[rs]
            wq_ref[ui, (2 * c + 1) * DN_CHUNK:(2 * c + 2) * DN_CHUNK, :] = qd16[rs]
        kd_ref[ui] = (kf[i] * jnp.exp(col[i][2] - col[i][1])).astype(BF16)


def _dn_scan_stage(tl, ptf_ref, ptb_ref, of_ref, ob_ref, s_ref, u_ref, wq_ref, kd_ref, aqk_ref):
    nchunk = tl // DN_CHUNK
    chains = [(d, h) for h in range(DN_HEADS) for d in range(2)]
    ids = [d * DN_HEADS + h for d, h in chains]
    for step in range(nchunk):
        rows = []
        for d, h in chains:
            c = step if d == 0 else nchunk - 1 - step
            rows.append(slice(c * DN_CHUNK, (c + 1) * DN_CHUNK))
        s = [s_ref[ui] for ui in ids]
        sb = [x.astype(BF16) for x in s]
        ws = [_dot(wq_ref[ui, 2 * rs.start:2 * rs.stop, :], b) for ui, rs, b in zip(ids, rows, sb)]
        v_new = [(u_ref[ui, rs, :] - x[:DN_CHUNK]).astype(BF16) for ui, rs, x in zip(ids, rows, ws)]
        o = [x[DN_CHUNK:] + _dot(aqk_ref[ui, rs, :], vn) for ui, rs, x, vn in zip(ids, rows, ws, v_new)]
        for (d, h), ui, rs, x, vn, oo in zip(chains, ids, rows, s, v_new, o):
            pt_ref = ptf_ref if d == 0 else ptb_ref
            li_tot = (4 + d) * DN_HEADS + h
            tot = pt_ref[li_tot:li_tot + 1, rs.start:rs.start + 1]
            s_ref[ui] = x * jnp.exp(tot) + _dot_tn(kd_ref[ui, rs, :], vn)
            (of_ref if d == 0 else ob_ref)[h, rs, :] = oo.astype(of_ref.dtype)


def _dn_kernel(tl, qf_ref, kf_ref, vf_ref, pf_ref, ptf_ref, qb_ref, kb_ref, vb_ref, pb_ref, ptb_ref,
               of_ref, ob_ref, s_ref, u_ref, wq_ref, kd_ref, aqk_ref):
    @pl.when(pl.program_id(1) == 0)
    def _():
        s_ref[...] = jnp.zeros_like(s_ref)

    r = lax.broadcasted_iota(I32, (tl, tl), 0)
    c = lax.broadcasted_iota(I32, (tl, tl), 1)
    same = (r // DN_CHUNK) == (c // DN_CHUNK)
    eye_wide = (lax.broadcasted_iota(I32, (DN_CHUNK, tl), 0) ==
                lax.broadcasted_iota(I32, (DN_CHUNK, tl), 1) % DN_CHUNK).astype(F32)
    masks_f = (same & (r >= c), same & (r > c), same, eye_wide)
    masks_b = (same & (r <= c), same & (r < c), same, eye_wide)
    refs_f = (qf_ref, kf_ref, vf_ref, pf_ref, ptf_ref)
    refs_b = (qb_ref, kb_ref, vb_ref, pb_ref, ptb_ref)

    units = [(d, h) for h in range(DN_HEADS) for d in range(2)]
    _dn_tile_stage(units, tl, refs_f, refs_b, masks_f, masks_b, u_ref, wq_ref, kd_ref, aqk_ref)
    _dn_scan_stage(tl, ptf_ref, ptb_ref, of_ref, ob_ref, s_ref, u_ref, wq_ref, kd_ref, aqk_ref)


def delta_rule(q, k, v, pack, packt):
    B, H, L, _ = q.shape
    tl = _tile(L, DN_TILE)
    nt = L // tl
    fwd = lambda b, i: (b, 0, i, 0)
    bwd = lambda b, i: (b, 0, nt - 1 - i, 0)
    hm_f = pl.BlockSpec((None, H, tl, DN_D), fwd)
    hm_b = pl.BlockSpec((None, H, tl, DN_D), bwd)
    pk_f = pl.BlockSpec((None, tl, LANES), lambda b, i: (b, i, 0))
    pk_b = pl.BlockSpec((None, tl, LANES), lambda b, i: (b, nt - 1 - i, 0))
    pt_f = pl.BlockSpec((None, LANES, tl), lambda b, i: (b, 0, i))
    pt_b = pl.BlockSpec((None, LANES, tl), lambda b, i: (b, 0, nt - 1 - i))
    o = jax.ShapeDtypeStruct((B, H, L, DN_D), BF16)
    return pl.pallas_call(
        functools.partial(_dn_kernel, tl),
        grid=(B, nt),
        in_specs=[hm_f, hm_f, hm_f, pk_f, pt_f, hm_b, hm_b, hm_b, pk_b, pt_b],
        out_specs=[hm_f, hm_b], out_shape=[o, o],
        scratch_shapes=[pltpu.VMEM((2 * H, DN_D, DN_D), F32),
                        pltpu.VMEM((2 * H, tl, DN_D), F32),
                        pltpu.VMEM((2 * H, 2 * tl, DN_D), BF16),
                        pltpu.VMEM((2 * H, tl, DN_D), BF16),
                        pltpu.VMEM((2 * H, tl, DN_CHUNK), BF16)],
        compiler_params=_params("parallel", "arbitrary"), name="delta_rule",
    )(q, k, v, pack, packt, q, k, v, pack, packt)


def _mem_attn_kernel(q_ref, kv_ref, o_ref):
    q = q_ref[...]
    kv = kv_ref[...]
    heads = range(MX_HEADS)
    ks = [slice(h * MX_HEAD_DIM, (h + 1) * MX_HEAD_DIM) for h in heads]
    vs = [slice(MX_W + h * MX_HEAD_DIM, MX_W + (h + 1) * MX_HEAD_DIM) for h in heads]
    s = [_dot_nt(kv[:, ks[h]], q[:, ks[h]]) for h in heads]
    m = [jnp.max(s[h], axis=0, keepdims=True) for h in heads]
    p = [jnp.exp2(s[h] - m[h]) for h in heads]
    den = [jnp.sum(p[h], axis=0, keepdims=True) for h in heads]
    o_t = [_dot_tn(kv[:, vs[h]], p[h].astype(BF16)) / den[h] for h in heads]
    o_ref[...] = jnp.concatenate([x.T for x in o_t], axis=1).astype(o_ref.dtype)


def mem_attention(qc, kv, tl_pref=1024):
    B, L, _ = qc.shape
    M = kv.shape[1]
    tl = _tile(L, tl_pref)
    return pl.pallas_call(
        _mem_attn_kernel, grid=(B, L // tl),
        in_specs=[pl.BlockSpec((None, tl, MX_W), lambda b, i: (b, i, 0)),
                  pl.BlockSpec((None, M, 2 * MX_W), lambda b, i: (b, 0, 0))],
        out_specs=pl.BlockSpec((None, tl, MX_W), lambda b, i: (b, i, 0)),
        out_shape=jax.ShapeDtypeStruct((B, L, MX_W), BF16),
        compiler_params=_params("parallel", "parallel"), name="mem_attention",
    )(qc, kv)


def _out_proj_a_kernel(of_ref, ob_ref, z_ref, xo_ref, x_ref, on_ref, w_ref, o_ref):
    parts = []
    z = z_ref[...].astype(F32)
    for h in range(DN_HEADS):
        o = of_ref[h].astype(F32) + ob_ref[h].astype(F32)
        y = _rmsnorm(o, on_ref[...]) * _silu(z[:, h * DN_D:(h + 1) * DN_D])
        parts.append(y.astype(BF16))
    parts.append(xo_ref[...])
    mixed = jnp.concatenate(parts, axis=1)
    o_ref[...] = x_ref[...] + _dot(mixed, w_ref[...])


def out_proj_a(o_f, o_b, z, xo, x, o_norm, w_out, tm_pref=512):
    B, H, L, _ = o_f.shape
    D = x.shape[-1]
    tm = _tile(L, tm_pref)
    hm = pl.BlockSpec((None, H, tm, DN_D), lambda b, i: (b, 0, i, 0))
    row = lambda w: pl.BlockSpec((None, tm, w), lambda b, i: (b, i, 0))
    return pl.pallas_call(
        _out_proj_a_kernel, grid=(B, L // tm),
        in_specs=[hm, hm, row(DN_V), row(MX_W), row(D),
                  pl.BlockSpec((1, DN_D), lambda b, i: (0, 0)),
                  pl.BlockSpec(w_out.shape, lambda b, i: (0, 0))],
        out_specs=row(D), out_shape=jax.ShapeDtypeStruct((B, L, D), F32),
        compiler_params=_params("parallel", "parallel"), name="out_proj_a",
    )(o_f, o_b, z, xo, x, o_norm.reshape(1, DN_D), w_out)


def _out_proj_b_kernel(o_ref, xo_ref, x_ref, w_ref, out_ref):
    mixed = jnp.concatenate([o_ref[...], xo_ref[...]], axis=1)
    out_ref[...] = x_ref[...] + _dot(mixed, w_ref[...])


def out_proj_b(o, xo, x, w_out, tm_pref=512):
    B, L, D = x.shape
    tm = _tile(L, tm_pref)
    row = lambda w: pl.BlockSpec((None, tm, w), lambda b, i: (b, i, 0))
    return pl.pallas_call(
        _out_proj_b_kernel, grid=(B, L // tm),
        in_specs=[row(WA_Q), row(MX_W), row(D), pl.BlockSpec(w_out.shape, lambda b, i: (0, 0))],
        out_specs=row(D), out_shape=jax.ShapeDtypeStruct((B, L, D), F32),
        compiler_params=_params("parallel", "parallel"), name="out_proj_b",
    )(o, xo, x, w_out)


def _swiglu_step(h, wg_ref, wu_ref, wd_ref, acc_ref):
    gate = _dot(h, wg_ref[...])
    up = _dot(h, wu_ref[...])
    acc_ref[...] += _dot((_silu(gate) * up).astype(BF16), wd_ref[...])


def _ffn_kernel(x_ref, g_ref, wg_ref, wu_ref, wd_ref, o_ref, h_ref, acc_ref):
    j = pl.program_id(1)

    @pl.when(j == 0)
    def _():
        h_ref[...] = _rmsnorm(x_ref[...], g_ref[...]).astype(BF16)
        acc_ref[...] = jnp.zeros_like(acc_ref)

    _swiglu_step(h_ref[...], wg_ref, wu_ref, wd_ref, acc_ref)

    @pl.when(j == pl.num_programs(1) - 1)
    def _():
        o_ref[...] = x_ref[...] + acc_ref[...]


def ffn(x, g, w_gate, w_up, w_down, tm_pref=1024):
    T, D = x.shape
    F = w_gate.shape[1]
    tm = _tile(T, tm_pref)
    tf = _tile(F, FF_TILE)
    return pl.pallas_call(
        _ffn_kernel, grid=(T // tm, F // tf),
        in_specs=[pl.BlockSpec((tm, D), lambda i, j: (i, 0)),
                  pl.BlockSpec((1, D), lambda i, j: (0, 0)),
                  pl.BlockSpec((D, tf), lambda i, j: (0, j)),
                  pl.BlockSpec((D, tf), lambda i, j: (0, j)),
                  pl.BlockSpec((tf, D), lambda i, j: (j, 0))],
        out_specs=pl.BlockSpec((tm, D), lambda i, j: (i, 0)),
        out_shape=jax.ShapeDtypeStruct((T, D), F32),
        scratch_shapes=[pltpu.VMEM((tm, D), BF16), pltpu.VMEM((tm, D), F32)],
        compiler_params=_params("parallel", "arbitrary"), name="ffn",
    )(x, g.reshape(1, D), w_gate, w_up, w_down)


def _win_attn_kernel(sink_ref, q_ref, kp_ref, kc_ref, kn_ref, vp_ref, vc_ref, vn_ref, bias_ref, o_ref):
    i = pl.program_id(1)
    n = pl.num_programs(1)
    qb = q_ref.shape[0]
    q = q_ref[...]
    k = jnp.concatenate([kp_ref[...], kc_ref[...], kn_ref[...]], axis=0)
    v = jnp.concatenate([vp_ref[...], vc_ref[...], vn_ref[...]], axis=0)
    bias = bias_ref[(i == 0).astype(I32) + 2 * (i == n - 1).astype(I32)]
    bias = jnp.concatenate([bias] * WA_GROUP, axis=1)
    cols = WA_GROUP * qb
    grp = lax.broadcasted_iota(I32, (1, cols), 1) // qb
    heads = range(WA_KV_HEADS)
    hs = [slice(h * WA_HEAD_DIM, (h + 1) * WA_HEAD_DIM) for h in heads]
    qs = [jnp.concatenate(
        [q[:, (h * WA_GROUP + g) * WA_HEAD_DIM:(h * WA_GROUP + g + 1) * WA_HEAD_DIM]
         for g in range(WA_GROUP)], axis=0) for h in heads]
    sk = []
    for h in heads:
        row = jnp.zeros((1, cols), F32)
        for g in range(WA_GROUP):
            row = jnp.where(grp == g, sink_ref[h * WA_GROUP + g] * LOG2_E, row)
        sk.append(row)
    s = [_dot_nt(k[:, hs[h]], qs[h]) + bias for h in heads]
    m = [jnp.maximum(jnp.max(s[h], axis=0, keepdims=True), sk[h]) for h in heads]
    p = [jnp.exp2(s[h] - m[h]) for h in heads]
    den = [jnp.sum(p[h], axis=0, keepdims=True) + jnp.exp2(sk[h] - m[h]) for h in heads]
    o = [(_dot_tn(v[:, hs[h]], p[h].astype(BF16)) / den[h]).T for h in heads]
    outs = [o[h][g * qb:(g + 1) * qb] for h in heads for g in range(WA_GROUP)]
    o_ref[...] = jnp.concatenate(outs, axis=1).astype(o_ref.dtype)


def _window_bias(qb):
    nk = qb + 2 * WA_BLOCK
    kj = jnp.arange(nk)[:, None]
    qi = jnp.arange(qb)[None, :]
    band = jnp.abs(kj - WA_BLOCK - qi) <= WA_BLOCK
    not_prev = kj >= WA_BLOCK
    not_next = kj < WA_BLOCK + qb
    variants = [band, band & not_prev, band & not_next, band & not_prev & not_next]
    return jnp.stack([jnp.where(m, 0.0, -jnp.inf).astype(F32) for m in variants])


def window_attention(q, k, v, sink):
    B, L, _ = q.shape
    qb = _tile(L, WA_QBLOCK)
    n = L // qb
    per = qb // WA_BLOCK
    nhalo = L // WA_BLOCK
    prev = lambda b, i, s: (b, jnp.maximum(i * per - 1, 0), 0)
    cur = lambda b, i, s: (b, i, 0)
    nxt = lambda b, i, s: (b, jnp.minimum((i + 1) * per, nhalo - 1), 0)
    halo = lambda f: pl.BlockSpec((None, WA_BLOCK, WA_KV), f)
    body = pl.BlockSpec((None, qb, WA_KV), cur)
    bias = _window_bias(qb)
    grid_spec = pltpu.PrefetchScalarGridSpec(
        num_scalar_prefetch=1, grid=(B, n),
        in_specs=[pl.BlockSpec((None, qb, WA_Q), cur),
                  halo(prev), body, halo(nxt), halo(prev), body, halo(nxt),
                  pl.BlockSpec(bias.shape, lambda b, i, s: (0, 0, 0))],
        out_specs=pl.BlockSpec((None, qb, WA_Q), cur))
    return pl.pallas_call(
        _win_attn_kernel, grid_spec=grid_spec,
        out_shape=jax.ShapeDtypeStruct((B, L, WA_Q), BF16),
        compiler_params=_params("parallel", "parallel"), name="window_attention",
    )(sink, q, k, k, k, v, v, v, bias)


def _rope_tables(L):
    inv = ROPE_THETA ** (-jnp.arange(0, WA_HEAD_DIM, 2, dtype=F32) / WA_HEAD_DIM)
    ang = jnp.arange(L, dtype=F32)[:, None] * inv[None, :]
    cos, sin = jnp.cos(ang), jnp.sin(ang)
    reps = LANES // WA_HEAD_DIM
    return jnp.tile(jnp.concatenate([cos, cos], axis=1), (1, reps)), \
        jnp.tile(jnp.concatenate([-sin, sin], axis=1), (1, reps))


def _router_kernel(tm, x_ref, g_ref, wr_ref, hs_ref, mi_ref, mf_ref, cnt_ref, before_ref, carry_ref):
    @pl.when(pl.program_id(0) == 0)
    def _():
        carry_ref[...] = jnp.zeros_like(carry_ref)

    before_ref[...] = carry_ref[...]
    hf = _rmsnorm(x_ref[...], g_ref[...])
    for s in range(TOK_TILES):
        hs_ref[pl.ds(s, tm, stride=TOK_TILES), :] = hf[:, s * LANES:(s + 1) * LANES]
    logits = _dot_nt_x3(wr_ref[...], hf)
    row = lax.broadcasted_iota(I32, logits.shape, 0)
    lg = jnp.where(row < N_EXPERTS, logits, -jnp.inf)
    m1 = jnp.max(lg, axis=0, keepdims=True)
    i1 = jnp.min(jnp.where(lg == m1, row, 2 * N_EXPERTS), axis=0, keepdims=True)
    lg2 = jnp.where(row == i1, -jnp.inf, lg)
    m2 = jnp.max(lg2, axis=0, keepdims=True)
    i2 = jnp.min(jnp.where(lg2 == m2, row, 2 * N_EXPERTS), axis=0, keepdims=True)
    e2 = jnp.exp(m2 - m1)
    g1 = 1.0 / (1.0 + e2)
    g2 = e2 / (1.0 + e2)
    oh1 = row == i1
    oh2 = row == i2
    oh = (oh1 | oh2).astype(F32)
    r = lax.broadcasted_iota(I32, (tm, tm), 0)
    c = lax.broadcasted_iota(I32, (tm, tm), 1)
    before = _dot(oh.astype(BF16), (r < c).astype(BF16)) + carry_ref[:, 0:1]
    rank1 = jnp.sum(jnp.where(oh1, before, 0.0), axis=0, keepdims=True).astype(I32)
    rank2 = jnp.sum(jnp.where(oh2, before, 0.0), axis=0, keepdims=True).astype(I32)
    carry_ref[...] += jnp.sum(oh, axis=1, keepdims=True)
    out_row = lax.broadcasted_iota(I32, mi_ref.shape, 0)
    mi_ref[...] = jnp.where(out_row == 0, i1, jnp.where(out_row == 1, i2, jnp.where(out_row == 2, rank1,
                            jnp.where(out_row == 3, rank2, 0))))
    mf_ref[...] = jnp.where(out_row == 0, g1, jnp.where(out_row == 1, g2, 0.0))
    cnt_ref[...] = carry_ref[...]


def moe_router(x, g, w_router, tm):
    T, D = x.shape
    rows = 2 * N_EXPERTS
    wr = jnp.zeros((rows, D), F32).at[:N_EXPERTS].set(w_router.T)
    return pl.pallas_call(
        functools.partial(_router_kernel, tm), grid=(T // tm,),
        in_specs=[pl.BlockSpec((tm, D), lambda i: (i, 0)),
                  pl.BlockSpec((1, D), lambda i: (0, 0)),
                  pl.BlockSpec((rows, D), lambda i: (0, 0))],
        out_specs=[pl.BlockSpec((tm * TOK_TILES, LANES), lambda i: (i, 0)),
                   pl.BlockSpec((SUBLANES, tm), lambda i: (0, i)),
                   pl.BlockSpec((SUBLANES, tm), lambda i: (0, i)),
                   pl.BlockSpec((rows, LANES), lambda i: (0, 0)),
                   pl.BlockSpec((rows, LANES), lambda i: (i, 0))],
        out_shape=[jax.ShapeDtypeStruct((T * TOK_TILES, LANES), F32),
                   jax.ShapeDtypeStruct((SUBLANES, T), I32),
                   jax.ShapeDtypeStruct((SUBLANES, T), F32),
                   jax.ShapeDtypeStruct((rows, LANES), F32),
                   jax.ShapeDtypeStruct((T // tm * rows, LANES), F32)],
        scratch_shapes=[pltpu.VMEM((rows, LANES), F32)],
        compiler_params=_params("arbitrary"), name="moe_router",
    )(x, g.reshape(1, D), wr)


def _row_copy(src, src_row, dst, dst_row, sem):
    return pltpu.make_async_copy(src.at[pl.ds(src_row * TOK_TILES, TOK_TILES), :],
                                 dst.at[pl.ds(dst_row * TOK_TILES, TOK_TILES), :], sem)


def _segment_copies(seg_ref, tile, stage_ref, hbm_ref, sem, to_hbm, wait):
    for e in range(N_EXPERTS):
        col = tile * N_EXPERTS + e
        row0, n, off = seg_ref[0, col], seg_ref[1, col], seg_ref[2, col]

        def piece(first, rows):
            a = stage_ref.at[pl.ds((off + first) * TOK_TILES, rows * TOK_TILES), :]
            b = hbm_ref.at[pl.ds((row0 + first) * TOK_TILES, rows * TOK_TILES), :]
            cp = pltpu.make_async_copy(a, b, sem) if to_hbm else pltpu.make_async_copy(b, a, sem)
            cp.wait() if wait else cp.start()

        def full(c, carry):
            piece(c * SEG_CHUNK, SEG_CHUNK)
            return carry

        lax.fori_loop(0, n // SEG_CHUNK, full, 0)
        rem = n % SEG_CHUNK
        for rows in (4, 2, 1):
            covered = (n - rem) + (rem - rem % (2 * rows))
            pl.when((rem & rows) != 0)(functools.partial(piece, covered, rows))


def _dispatch_kernel(tm, seg_ref, tail_ref, pos_ref, hs_ref, xs_ref, stage0_ref, stage1_ref, zero_ref,
                     sem0, sem1, zero_sem):
    i = pl.program_id(0)
    last = pl.num_programs(0) - 1

    @pl.when(i == 0)
    def _():
        zero_ref[...] = jnp.zeros_like(zero_ref)
        for wait in (False, True):
            for e in range(N_EXPERTS):
                def pad_row(r, carry):
                    cp = _row_copy(zero_ref, 0, xs_ref, tail_ref[0, e] + r, zero_sem)
                    cp.wait() if wait else cp.start()
                    return carry

                lax.fori_loop(0, tail_ref[1, e], pad_row, 0)

            def pad_chunk(r, carry):
                row = tail_ref[0, N_EXPERTS] + r * ZERO_ROWS
                cp = pltpu.make_async_copy(
                    zero_ref, xs_ref.at[pl.ds(row * TOK_TILES, ZERO_ROWS * TOK_TILES), :], zero_sem)
                cp.wait() if wait else cp.start()
                return carry

            lax.fori_loop(0, tail_ref[1, N_EXPERTS], pad_chunk, 0)

    def run(stage_ref, sem, other_ref, other_sem):
        @pl.when(i >= 2)
        def _():
            _segment_copies(seg_ref, i - 2, stage_ref, xs_ref, sem, True, True)

        def place(t, carry):
            row = hs_ref[pl.ds(t * TOK_TILES, TOK_TILES), :]
            for k in range(TOP_K):
                stage_ref[pl.ds(pos_ref[k, t] * TOK_TILES, TOK_TILES), :] = row
            return carry

        lax.fori_loop(0, tm, place, 0, unroll=8)
        _segment_copies(seg_ref, i, stage_ref, xs_ref, sem, True, False)

        @pl.when(i == last)
        def _():
            @pl.when(i >= 1)
            def _():
                _segment_copies(seg_ref, i - 1, other_ref, xs_ref, other_sem, True, True)

            _segment_copies(seg_ref, i, stage_ref, xs_ref, sem, True, True)

    pl.when(i % 2 == 0)(functools.partial(run, stage0_ref, sem0, stage1_ref, sem1))
    pl.when(i % 2 == 1)(functools.partial(run, stage1_ref, sem1, stage0_ref, sem0))


def moe_dispatch(hs, pos, seg, tail, n_rows, tm):
    T = pos.shape[1]
    stage = pltpu.VMEM((TOP_K * tm * TOK_TILES, LANES), F32)
    grid_spec = pltpu.PrefetchScalarGridSpec(
        num_scalar_prefetch=2, grid=(T // tm,),
        in_specs=[pl.BlockSpec((TOP_K, tm), lambda i, sg, tl: (0, i), memory_space=pltpu.SMEM),
                  pl.BlockSpec((tm * TOK_TILES, LANES), lambda i, sg, tl: (i, 0))],
        out_specs=pl.BlockSpec(memory_space=pl.ANY),
        scratch_shapes=[stage, stage, pltpu.VMEM((ZERO_ROWS * TOK_TILES, LANES), F32),
                        pltpu.SemaphoreType.DMA(()), pltpu.SemaphoreType.DMA(()),
                        pltpu.SemaphoreType.DMA(())])
    return pl.pallas_call(
        functools.partial(_dispatch_kernel, tm), grid_spec=grid_spec,
        out_shape=jax.ShapeDtypeStruct((n_rows * TOK_TILES, LANES), F32),
        compiler_params=_params("arbitrary"), name="moe_dispatch",
    )(seg, tail, pos, hs)


def _expert_kernel(tm, be_ref, nu_ref, xs_ref, wg_ref, wu_ref, wd_ref, ys_ref, h_ref, acc_ref):
    b = pl.program_id(0)
    j = pl.program_id(1)
    last = pl.num_programs(1) - 1
    used = b < nu_ref[0]

    @pl.when(used & (j == 0))
    def _():
        for s in range(TOK_TILES):
            h_ref[:, s * LANES:(s + 1) * LANES] = xs_ref[pl.ds(s, tm, stride=TOK_TILES), :].astype(BF16)
        acc_ref[...] = jnp.zeros_like(acc_ref)

    @pl.when(used)
    def _():
        _swiglu_step(h_ref[...], wg_ref, wu_ref, wd_ref, acc_ref)

    @pl.when(used & (j == last))
    def _():
        for s in range(TOK_TILES):
            ys_ref[pl.ds(s, tm, stride=TOK_TILES), :] = acc_ref[:, s * LANES:(s + 1) * LANES]

    @pl.when(jnp.logical_not(used) & (j == last))
    def _():
        ys_ref[...] = jnp.zeros_like(ys_ref)


def moe_experts(xs, blk_expert, n_used, w_gate, w_up, w_down, tm):
    E, D, F = w_gate.shape
    nb = xs.shape[0] // (tm * TOK_TILES)
    tf = _tile(F, FF_TILE)
    nj = F // tf
    jj = lambda b, j, be, nu: jnp.where(b < nu[0], j, nj - 1)
    grid_spec = pltpu.PrefetchScalarGridSpec(
        num_scalar_prefetch=2, grid=(nb, nj),
        in_specs=[pl.BlockSpec((tm * TOK_TILES, LANES), lambda b, j, be, nu: (jnp.minimum(b, nu[0] - 1), 0)),
                  pl.BlockSpec((None, D, tf), lambda b, j, be, nu: (be[b], 0, jj(b, j, be, nu))),
                  pl.BlockSpec((None, D, tf), lambda b, j, be, nu: (be[b], 0, jj(b, j, be, nu))),
                  pl.BlockSpec((None, tf, D), lambda b, j, be, nu: (be[b], jj(b, j, be, nu), 0))],
        out_specs=pl.BlockSpec((tm * TOK_TILES, LANES), lambda b, j, be, nu: (b, 0)),
        scratch_shapes=[pltpu.VMEM((tm, D), BF16), pltpu.VMEM((tm, D), F32)])
    return pl.pallas_call(
        functools.partial(_expert_kernel, tm), grid_spec=grid_spec,
        out_shape=jax.ShapeDtypeStruct(xs.shape, F32),
        compiler_params=_params("arbitrary", "arbitrary"), name="moe_experts",
    )(blk_expert, n_used, xs, w_gate, w_up, w_down)


def _combine_kernel(tm, seg_ref, pos_ref, gate_ref, ys_ref, x_ref, g_ref, o_ref,
                    stage0_ref, stage1_ref, ybuf_ref, y_ref, sem0, sem1):
    i = pl.program_id(0)
    n = pl.num_programs(0)

    def run(stage_ref, sem, other_ref, other_sem):
        @pl.when(i == 0)
        def _():
            _segment_copies(seg_ref, i, stage_ref, ys_ref, sem, False, False)

        @pl.when(i + 1 < n)
        def _():
            _segment_copies(seg_ref, i + 1, other_ref, ys_ref, other_sem, False, False)

        _segment_copies(seg_ref, i, stage_ref, ys_ref, sem, False, True)

        def pick(t, carry):
            a = stage_ref[pl.ds(pos_ref[0, t] * TOK_TILES, TOK_TILES), :]
            b = stage_ref[pl.ds(pos_ref[1, t] * TOK_TILES, TOK_TILES), :]
            ybuf_ref[pl.ds(t * TOK_TILES, TOK_TILES), :] = a * gate_ref[0, t] + b * gate_ref[1, t]
            return carry

        lax.fori_loop(0, tm, pick, 0, unroll=8)

    pl.when(i % 2 == 0)(functools.partial(run, stage0_ref, sem0, stage1_ref, sem1))
    pl.when(i % 2 == 1)(functools.partial(run, stage1_ref, sem1, stage0_ref, sem0))
    for s in range(TOK_TILES):
        y_ref[:, s * LANES:(s + 1) * LANES] = ybuf_ref[pl.ds(s, tm, stride=TOK_TILES), :]
    o_ref[...] = _rmsnorm(x_ref[...] + y_ref[...], g_ref[...])


def moe_combine(ys, pos, gates, seg, x, g_final, tm):
    T, D = x.shape
    stage = pltpu.VMEM((TOP_K * tm * TOK_TILES, LANES), F32)
    grid_spec = pltpu.PrefetchScalarGridSpec(
        num_scalar_prefetch=1, grid=(T // tm,),
        in_specs=[pl.BlockSpec((TOP_K, tm), lambda i, sg: (0, i), memory_space=pltpu.SMEM),
                  pl.BlockSpec((TOP_K, tm), lambda i, sg: (0, i), memory_space=pltpu.SMEM),
                  pl.BlockSpec(memory_space=pl.ANY),
                  pl.BlockSpec((tm, D), lambda i, sg: (i, 0)),
                  pl.BlockSpec((1, D), lambda i, sg: (0, 0))],
        out_specs=pl.BlockSpec((tm, D), lambda i, sg: (i, 0)),
        scratch_shapes=[stage, stage, pltpu.VMEM((tm * TOK_TILES, LANES), F32), pltpu.VMEM((tm, D), F32),
                        pltpu.SemaphoreType.DMA(()), pltpu.SemaphoreType.DMA(())])
    return pl.pallas_call(
        functools.partial(_combine_kernel, tm), grid_spec=grid_spec,
        out_shape=jax.ShapeDtypeStruct((T, D), F32),
        compiler_params=_params("arbitrary"), name="moe_combine",
    )(seg, pos, gates, ys, x, g_final.reshape(1, D))


def moe_layer(x, g_ffn, g_final, w_router, w_gate, w_up, w_down):
    T = x.shape[0]
    tm = MOE_ROWS_LARGE if T * TOP_K >= MOE_ROWS_LARGE * N_EXPERTS * 4 else MOE_ROWS_SMALL
    tt = _tile(T, MOE_TILE)
    hs, meta_i, meta_f, cnt, before = moe_router(x, g_ffn, w_router, tt)
    counts = cnt[:N_EXPERTS, 0].astype(I32)
    padded = (counts + tm - 1) // tm * tm
    pad_end = jnp.cumsum(padded)
    pad_start = pad_end - padded
    idx = meta_i[0:TOP_K]
    rank = meta_i[TOP_K:2 * TOP_K]
    before = before.reshape(T // tt, 2 * N_EXPERTS, LANES)[:, :N_EXPERTS, 0].astype(I32)
    seg_len = jnp.concatenate([before[1:], counts[None]], axis=0) - before
    stage_off = jnp.cumsum(seg_len, axis=1) - seg_len
    seg = jnp.stack([(pad_start[None] + before).reshape(-1), seg_len.reshape(-1),
                     stage_off.reshape(-1)]).astype(I32)
    in_tile = jnp.repeat(stage_off - before, tt, axis=0)
    chosen = idx[..., None] == jnp.arange(N_EXPERTS, dtype=I32)
    pos = (rank + jnp.sum(jnp.where(chosen, in_tile[None], 0), axis=-1)).astype(I32)
    gates = meta_f[0:TOP_K]
    nb = (T * TOP_K) // tm + N_EXPERTS
    n_used = (pad_end[-1] // tm).astype(I32)
    blk_row = jnp.minimum(jnp.arange(nb, dtype=I32), n_used - 1) * tm
    blk_expert = jnp.minimum(jnp.searchsorted(pad_end, blk_row, side='right'), N_EXPERTS - 1).astype(I32)
    tail = jnp.stack([jnp.append(pad_start + counts, pad_end[-1]),
                      jnp.append(padded - counts, (nb * tm - pad_end[-1]) // ZERO_ROWS)]).astype(I32)
    xs = moe_dispatch(hs, pos, seg, tail, nb * tm, tt)
    ys = moe_experts(xs, blk_expert, n_used.reshape(1), w_gate, w_up, w_down, tm)
    return moe_combine(ys, pos, gates, seg, x, g_final, tt)


def _prep_weights(g_mix, g_mem, g_ffn, g_final, a_w_in, a_conv, a_log, a_dt_bias, a_o_norm,
                  a_mem_kv, a_w_out, b_w_in, b_sink, b_mem_kv, b_w_out, f_w_gate, f_w_up, f_w_down,
                  e_router, e_w_gate, e_w_up, e_w_down):
    s1 = DN_QKV
    s2 = s1 + DN_V
    s3 = s2 + 4 * DN_HEADS
    w = a_w_in[0]
    gates_w = jnp.zeros((GATE_ROWS, D_MODEL), F32).at[:4 * DN_HEADS].set(w[:, s2:s3].T)
    wb = b_w_in[0]
    return dict(
        g_mix=g_mix, g_mem=g_mem, g_ffn=g_ffn, g_final=g_final,
        a_in=[w[:, :s1].astype(BF16), w[:, s1:s2].astype(BF16), w[:, s3:].astype(BF16), gates_w],
        a_conv=a_conv[0], a_log=a_log[0], a_dt_bias=a_dt_bias[0], a_o_norm=a_o_norm[0],
        a_mem_kv=a_mem_kv[0].astype(BF16), a_w_out=a_w_out[0].astype(BF16),
        b_in=[wb[:, :WA_Q].astype(BF16), wb[:, WA_Q:WA_Q + WA_KV].astype(BF16),
              wb[:, WA_Q + WA_KV:WA_Q + 2 * WA_KV].astype(BF16), wb[:, WA_Q + 2 * WA_KV:].astype(BF16)],
        b_sink=b_sink[0], b_mem_kv=b_mem_kv[0].astype(BF16), b_w_out=b_w_out[0].astype(BF16),
        f_w_gate=f_w_gate[0].astype(BF16), f_w_up=f_w_up[0].astype(BF16), f_w_down=f_w_down[0].astype(BF16),
        e_router=e_router[0], e_w_gate=e_w_gate[0].astype(BF16), e_w_up=e_w_up[0].astype(BF16),
        e_w_down=e_w_down[0].astype(BF16))


def _trunk(x, mem, p):
    B, L, D = x.shape
    M = mem.shape[1]
    T = B * L
    x2 = x.reshape(T, D)
    mem2 = mem.reshape(B * M, D)

    qkv, z, qc, gates = norm_proj(x2, p['g_mix'][0], p['a_in'], [BF16, BF16, BF16, F32],
                                  scales=(None, None, MX_HEAD_DIM ** -0.5 * LOG2_E, None))
    (kv,) = norm_proj(mem2, p['g_mem'][0], [p['a_mem_kv']], [BF16])
    q, k, v, pack, packt = dn_prep(qkv.reshape(B, L, DN_QKV), gates,
                                   p['a_conv'], p['a_log'], p['a_dt_bias'])
    o_f, o_b = delta_rule(q, k, v, pack, packt)
    xo = mem_attention(qc.reshape(B, L, MX_W), kv.reshape(B, M, 2 * MX_W))
    x1 = out_proj_a(o_f, o_b, z.reshape(B, L, DN_V), xo, x, p['a_o_norm'], p['a_w_out'])
    x1 = ffn(x1.reshape(T, D), p['g_ffn'][0], p['f_w_gate'], p['f_w_up'], p['f_w_down'])

    cos, sin_signed = _rope_tables(L)
    q, k, v, qc = norm_proj(x1, p['g_mix'][1], p['b_in'], [BF16, BF16, BF16, BF16],
                            scales=(WA_HEAD_DIM ** -0.5 * LOG2_E, None, None, MX_HEAD_DIM ** -0.5 * LOG2_E),
                            rope=(cos, sin_signed, (True, True, False, False)))
    (kv,) = norm_proj(mem2, p['g_mem'][1], [p['b_mem_kv']], [BF16])
    o = window_attention(q.reshape(B, L, WA_Q), k.reshape(B, L, WA_KV), v.reshape(B, L, WA_KV),
                         p['b_sink'])
    xo = mem_attention(qc.reshape(B, L, MX_W), kv.reshape(B, M, 2 * MX_W))
    x2 = out_proj_b(o, xo, x1.reshape(B, L, D), p['b_w_out'])
    y = moe_layer(x2.reshape(T, D), p['g_ffn'][1], p['g_final'], p['e_router'],
                  p['e_w_gate'], p['e_w_up'], p['e_w_down'])
    return y.reshape(B, L, D)


def kernel(x_prompt, x_sample, mem_prompt, mem_sample, g_mix, g_mem, g_ffn, g_final, a_w_in, a_conv,
           a_log, a_dt_bias, a_o_norm, a_mem_kv, a_w_out, b_w_in, b_sink, b_mem_kv, b_w_out,
           f_w_gate, f_w_up, f_w_down, e_router, e_w_gate, e_w_up, e_w_down):
    p = _prep_weights(g_mix, g_mem, g_ffn, g_final, a_w_in, a_conv, a_log, a_dt_bias, a_o_norm,
                      a_mem_kv, a_w_out, b_w_in, b_sink, b_mem_kv, b_w_out, f_w_gate, f_w_up, f_w_down,
                      e_router, e_w_gate, e_w_up, e_w_down)
    return (_trunk(x_prompt, mem_prompt, p), _trunk(x_sample, mem_sample, p))
```
